```python
import math
import jax, jax.numpy as jnp
from jax import lax
import numpy as np

D_MODEL = 1024
BATCH = 4
SEQ = 4096
DEPTH = 1
DEC_BATCH = 128
DEC_SEQ = 8
PAST_LEN = 8192
PAGE_SIZE = 128

MIX_A = 512
A_GROUPS = 4
A_GW = MIX_A // A_GROUPS
CHUNK = 128
N_HEADS = 8
N_KV = 2
HEAD_DIM = 64
GQA = N_HEADS // N_KV
MIX_B = N_HEADS * HEAD_DIM
WINDOW = 128
IN_COLS = 2 * MIX_A + (N_HEADS + 2 * N_KV) * HEAD_DIM
REL_BUCKETS = 32
REL_MAX_EXACT = REL_BUCKETS // 2
REL_MAX_DIST = 128
N_MEM = 256
MEM_HEADS = 4
MEM_HEAD_DIM = 128
MEM_INNER = MEM_HEADS * MEM_HEAD_DIM
PEER_HEADS = 8
PEER_KEYS = 128
PEER_N = PEER_KEYS * PEER_KEYS
PEER_HALF = 128
PEER_QDIM = 2 * PEER_HALF
PEER_TOPK = 16
PEER_BLOCK = 128
EPS = 1e-6

kernel_name = 'hymba_gmlp_swa_peer_step'


def rms_norm(x, g):
    xf = x.astype(jnp.float32)
    y = xf * lax.rsqrt(jnp.mean(xf * xf, axis=-1, keepdims=True) + EPS)
    return (y * g.astype(jnp.float32)).astype(x.dtype)


def layer_norm_groups(v, g, b):
    vf = v.astype(jnp.float32).reshape(v.shape[:-1] + (A_GROUPS, A_GW))
    mu = jnp.mean(vf, axis=-1, keepdims=True)
    var = jnp.mean(jnp.square(vf - mu), axis=-1, keepdims=True)
    y = (vf - mu) * lax.rsqrt(var + EPS)
    y = y * g.astype(jnp.float32).reshape(A_GROUPS, A_GW) + b.astype(jnp.float32).reshape(A_GROUPS, A_GW)
    return y.astype(v.dtype)


def t5_bucket(d):
    n = jnp.maximum(d, 0)
    nf = jnp.maximum(n, 1).astype(jnp.float32)
    large = REL_MAX_EXACT + (jnp.log(nf / REL_MAX_EXACT) / math.log(REL_MAX_DIST / REL_MAX_EXACT)
                             * (REL_BUCKETS - REL_MAX_EXACT)).astype(jnp.int32)
    large = jnp.minimum(large, REL_BUCKETS - 1)
    return jnp.where(n < REL_MAX_EXACT, n, large)


def rel_bias_logits(d, rel_bias):
    b = rel_bias[t5_bucket(d)]
    return jnp.moveaxis(b, -1, 0).reshape((N_KV, GQA) + d.shape).astype(jnp.float32)


def attend_sink(q, k, v, bias, mask, sink):
    s = jnp.einsum('...qkgd,...skd->...kgqs', q, k).astype(jnp.float32) / math.sqrt(HEAD_DIM) + bias
    s = jnp.where(mask, s, -1e30)
    sk = sink.astype(jnp.float32).reshape(N_KV, GQA, 1, 1)
    m = jnp.maximum(jnp.max(s, axis=-1, keepdims=True), sk)
    p = jnp.exp(s - m)
    p = p / (jnp.sum(p, axis=-1, keepdims=True) + jnp.exp(sk - m))
    return jnp.einsum('...kgqs,...skd->...qkgd', p.astype(v.dtype), v)


def mix_inputs(x, norm_g, w_in, ln_g, ln_b):
    xn = rms_norm(x, norm_g)
    h = xn @ w_in
    z = jax.nn.gelu(h[..., :2 * MIX_A])
    u = z[..., :MIX_A]
    v = layer_norm_groups(z[..., MIX_A:], ln_g, ln_b)
    r = h[..., 2 * MIX_A:]
    lead = x.shape[:-1]
    q = r[..., :MIX_B]
    k = r[..., MIX_B:MIX_B + N_KV * HEAD_DIM].reshape(lead + (N_KV, HEAD_DIM))
    vb = r[..., MIX_B + N_KV * HEAD_DIM:].reshape(lead + (N_KV, HEAD_DIM))
    return u, v, q, k, vb


def spatial_gate_prompt(v, w_s, b_s):
    B, S = v.shape[:2]
    vc = v.reshape(B, S // CHUNK, CHUNK, A_GROUPS, A_GW)
    wm = w_s * jnp.tril(jnp.ones((CHUNK, CHUNK), w_s.dtype))
    out = jnp.einsum('gts,bcsgd->bctgd', wm, vc) + b_s.T[None, None, :, :, None]
    return out.reshape(B, S, A_GROUPS, A_GW)


def spatial_gate_sample(v, w_s, b_s):
    L = v.shape[1]
    wm = (w_s * jnp.tril(jnp.ones((CHUNK, CHUNK), w_s.dtype)))[:, :L, :L]
    return jnp.einsum('gts,bsgd->btgd', wm, v) + b_s[:, :L].T[None, :, :, None]


def swa_prompt(q, k, v, rel_bias, sink):
    B, S = q.shape[:2]
    nb = S // WINDOW
    qb = q.reshape(B, nb, WINDOW, N_KV, GQA, HEAD_DIM)
    kb = k.reshape(B, nb, WINDOW, N_KV, HEAD_DIM)
    vb = v.reshape(B, nb, WINDOW, N_KV, HEAD_DIM)
    padw = ((0, 0), (1, 0), (0, 0), (0, 0), (0, 0))
    kk = jnp.concatenate([jnp.pad(kb[:, :-1], padw), kb], axis=2)
    vv = jnp.concatenate([jnp.pad(vb[:, :-1], padw), vb], axis=2)
    qi = jnp.arange(WINDOW)[:, None] + WINDOW
    kj = jnp.arange(2 * WINDOW)[None, :]
    d = qi - kj
    band = (d >= 0) & (d < WINDOW)
    mask = band[None] & ((jnp.arange(nb)[:, None, None] > 0) | (kj[None] >= WINDOW))
    o = attend_sink(qb, kk, vv, rel_bias_logits(d, rel_bias), mask[None, :, None, None], sink)
    return o.reshape(B, S, MIX_B)


def swa_sample(q, k_new, v_new, k_buf, v_buf, rel_bias, sink):
    N, L = q.shape[:2]
    wb = k_buf.shape[1]
    kk = jnp.concatenate([k_buf, k_new], axis=1)
    vv = jnp.concatenate([v_buf, v_new], axis=1)
    d = (wb + jnp.arange(L))[:, None] - jnp.arange(wb + L)[None, :]
    mask = (d >= 0) & (d < WINDOW)
    qh = q.reshape(N, L, N_KV, GQA, HEAD_DIM)
    o = attend_sink(qh, kk, vv, rel_bias_logits(d, rel_bias), mask, sink)
    return o.reshape(N, L, MIX_B), kk[:, -wb:], vv[:, -wb:]


def mix_outputs(x, u, s, o_b, g_a, g_b, w_out):
    y_a = u * s.reshape(u.shape)
    y = jnp.concatenate([rms_norm(y_a, g_a), rms_norm(o_b, g_b)], axis=-1) @ w_out
    return x + y


def memory_kv(mem, g_src, w_mk, w_mv):
    N, M = mem.shape[:2]
    mn = rms_norm(mem, g_src)
    mk = (mn @ w_mk).reshape(N, M, MEM_HEADS, MEM_HEAD_DIM)
    mv = (mn @ w_mv).reshape(N, M, MEM_HEADS, MEM_HEAD_DIM)
    return mk, mv


def memory_attn(x, g, w_mq, w_mo, mk, mv):
    N, L = x.shape[:2]
    q = (rms_norm(x, g) @ w_mq).reshape(N, L, MEM_HEADS, MEM_HEAD_DIM)
    s = jnp.einsum('nlhd,nmhd->nhlm', q, mk).astype(jnp.float32) / math.sqrt(MEM_HEAD_DIM)
    p = jax.nn.softmax(s, axis=-1).astype(mv.dtype)
    o = jnp.einsum('nhlm,nmhd->nlhd', p, mv).reshape(N, L, MEM_INNER)
    return x + o @ w_mo


def peer_ffn(x, g, w_q, keys, e_u, e_v):
    N, L, D = x.shape
    xn = rms_norm(x, g).reshape(N * L, D)
    T = N * L
    nblk = -(-T // PEER_BLOCK)
    xb = jnp.pad(xn, ((0, nblk * PEER_BLOCK - T), (0, 0))).reshape(nblk, PEER_BLOCK, D)

    def one_block(xt):
        q = (xt @ w_q).reshape(PEER_BLOCK, PEER_HEADS, 2, PEER_HALF)
        s = jnp.einsum('thpc,hpnc->thpn', q, keys).astype(jnp.float32)
        s1, i1 = lax.top_k(s[:, :, 0], PEER_TOPK)
        s2, i2 = lax.top_k(s[:, :, 1], PEER_TOPK)
        cand = (s1[..., :, None] + s2[..., None, :]).reshape(PEER_BLOCK, PEER_HEADS, PEER_TOPK * PEER_TOPK)
        cidx = (i1[..., :, None] * PEER_KEYS + i2[..., None, :]).reshape(PEER_BLOCK, PEER_HEADS, PEER_TOPK * PEER_TOPK)
        sc, pos = lax.top_k(cand, PEER_TOPK)
        eidx = jnp.take_along_axis(cidx, pos, axis=-1)
        gate = jax.nn.softmax(sc, axis=-1)
        act = jax.nn.gelu(jnp.einsum('thkd,td->thk', e_u[eidx], xt).astype(jnp.float32))
        w = (gate * act).astype(xt.dtype)
        return jnp.einsum('thk,thkd->td', w, e_v[eidx])

    out = lax.map(one_block, xb).reshape(nblk * PEER_BLOCK, D)[:T].reshape(N, L, D)
    return x + out


def setup_inputs(seed: int = 0) -> dict:
    key = jax.random.key(seed)
    ks = jax.random.split(key, 32)
    f32 = jnp.float32

    def nrm(k, shape, scale):
        return jax.random.normal(k, shape, f32) * scale

    def gain(k, shape):
        return 1.0 + 0.01 * jax.random.normal(k, shape, f32)

    w_buf = min(WINDOW, PAST_LEN)
    D = D_MODEL
    return {
        'x_prompt': nrm(ks[0], (BATCH, SEQ, D), 1.0),
        'x_sample': nrm(ks[1], (DEC_BATCH, DEC_SEQ, D), 1.0),
        'mem_prompt': nrm(ks[2], (BATCH, N_MEM, D), 1.0),
        'cache_swa_k': nrm(ks[3], (DEPTH, DEC_BATCH, w_buf, N_KV, HEAD_DIM), 1.0),
        'cache_swa_v': nrm(ks[4], (DEPTH, DEC_BATCH, w_buf, N_KV, HEAD_DIM), 1.0),
        'cache_mem_k': nrm(ks[5], (DEPTH, DEC_BATCH, N_MEM, MEM_HEADS, MEM_HEAD_DIM), 1.0),
        'cache_mem_v': nrm(ks[6], (DEPTH, DEC_BATCH, N_MEM, MEM_HEADS, MEM_HEAD_DIM), 1.0),
        'norm_mix_g': gain(ks[7], (DEPTH, D)),
        'w_in': nrm(ks[8], (DEPTH, D, IN_COLS), D ** -0.5),
        'ln_v_g': gain(ks[9], (DEPTH, MIX_A)),
        'ln_v_b': nrm(ks[10], (DEPTH, MIX_A), 0.01),
        'spatial_w': nrm(ks[11], (DEPTH, A_GROUPS, CHUNK, CHUNK), CHUNK ** -0.5),
        'spatial_b': gain(ks[12], (DEPTH, A_GROUPS, CHUNK)),
        'attn_sinks': nrm(ks[13], (DEPTH, N_HEADS), 0.5),
        'rel_bias': nrm(ks[14], (REL_BUCKETS, N_HEADS), 0.5),
        'norm_a_out_g': gain(ks[15], (DEPTH, MIX_A)),
        'norm_b_out_g': gain(ks[16], (DEPTH, MIX_B)),
        'w_out': nrm(ks[17], (DEPTH, MIX_A + MIX_B, D), (MIX_A + MIX_B) ** -0.5),
        'norm_mem_g': gain(ks[18], (DEPTH, D)),
        'norm_memsrc_g': gain(ks[19], (DEPTH, D)),
        'w_mq': nrm(ks[20], (DEPTH, D, MEM_INNER), D ** -0.5),
        'w_mk': nrm(ks[21], (DEPTH, D, MEM_INNER), D ** -0.5),
        'w_mv': nrm(ks[22], (DEPTH, D, MEM_INNER), D ** -0.5),
        'w_mo': nrm(ks[23], (DEPTH, MEM_INNER, D), MEM_INNER ** -0.5),
        'norm_ffn_g': gain(ks[24], (DEPTH, D)),
        'peer_wq': nrm(ks[25], (DEPTH, D, PEER_HEADS * PEER_QDIM), D ** -0.5),
        'peer_keys': nrm(ks[26], (DEPTH, PEER_HEADS, 2, PEER_KEYS, PEER_HALF), PEER_HALF ** -0.5),
        'peer_u': nrm(ks[27], (DEPTH, PEER_N, D), D ** -0.5),
        'peer_v': nrm(ks[28], (DEPTH, PEER_N, D), PEER_HEADS ** -0.5),
        'norm_final_g': gain(ks[29], (D,)),
    }


def reference(x_prompt, x_sample, mem_prompt, cache_swa_k, cache_swa_v, cache_mem_k, cache_mem_v,
              norm_mix_g, w_in, ln_v_g, ln_v_b, spatial_w, spatial_b, attn_sinks, rel_bias,
              norm_a_out_g, norm_b_out_g, w_out, norm_mem_g, norm_memsrc_g, w_mq, w_mk, w_mv, w_mo,
              norm_ffn_g, peer_wq, peer_keys, peer_u, peer_v, norm_final_g):
    hp, hs = x_prompt, x_sample
    swa_k_p, swa_v_p, mem_k_p, mem_v_p = [], [], [], []
    swa_k_s, swa_v_s, chunk_v_s = [], [], []
    for l in range(DEPTH):
        u, v, q, k, vb = mix_inputs(hp, norm_mix_g[l], w_in[l], ln_v_g[l], ln_v_b[l])
        s_gate = spatial_gate_prompt(v, spatial_w[l], spatial_b[l])
        o_b = swa_prompt(q, k, vb, rel_bias, attn_sinks[l])
        hp = mix_outputs(hp, u, s_gate, o_b, norm_a_out_g[l], norm_b_out_g[l], w_out[l])
        swa_k_p.append(k[:, -WINDOW:])
        swa_v_p.append(vb[:, -WINDOW:])
        mk, mv = memory_kv(mem_prompt, norm_memsrc_g[l], w_mk[l], w_mv[l])
        mem_k_p.append(mk)
        mem_v_p.append(mv)
        hp = memory_attn(hp, norm_mem_g[l], w_mq[l], w_mo[l], mk, mv)
        hp = peer_ffn(hp, norm_ffn_g[l], peer_wq[l], peer_keys[l], peer_u[l], peer_v[l])

        u, v, q, k, vb = mix_inputs(hs, norm_mix_g[l], w_in[l], ln_v_g[l], ln_v_b[l])
        s_gate = spatial_gate_sample(v, spatial_w[l], spatial_b[l])
        o_b, kbuf, vbuf = swa_sample(q, k, vb, cache_swa_k[l], cache_swa_v[l], rel_bias, attn_sinks[l])
        hs = mix_outputs(hs, u, s_gate, o_b, norm_a_out_g[l], norm_b_out_g[l], w_out[l])
        swa_k_s.append(kbuf)
        swa_v_s.append(vbuf)
        chunk_v_s.append(v)
        hs = memory_attn(hs, norm_mem_g[l], w_mq[l], w_mo[l], cache_mem_k[l], cache_mem_v[l])
        hs = peer_ffn(hs, norm_ffn_g[l], peer_wq[l], peer_keys[l], peer_u[l], peer_v[l])

    y_prompt = rms_norm(hp, norm_final_g)
    y_sample = rms_norm(hs, norm_final_g)
    return (y_prompt, y_sample, jnp.stack(swa_k_p), jnp.stack(swa_v_p), jnp.stack(mem_k_p), jnp.stack(mem_v_p),
            jnp.stack(swa_k_s), jnp.stack(swa_v_s), jnp.stack(chunk_v_s))
```

```python
import functools
import math

import jax
import jax.numpy as jnp
from jax import lax
from jax.experimental import pallas as pl
from jax.experimental.pallas import tpu as pltpu

F32 = jnp.float32
BF16 = jnp.bfloat16

D_MODEL = 1024
MIX_A = 512
A_GROUPS = 4
A_GW = 128
CHUNK = 128
N_HEADS = 8
N_KV = 2
HEAD_DIM = 64
GQA = 4
MIX_B = 512
WINDOW = 128
IN_COLS = 1792
REL_BUCKETS = 32
REL_MAX_EXACT = 16
REL_MAX_DIST = 128
N_MEM = 256
MEM_HEADS = 4
MEM_HEAD_DIM = 128
MEM_INNER = 512
PEER_HEADS = 8
PEER_KEYS = 128
PEER_N = PEER_KEYS * PEER_KEYS
PEER_HALF = 128
PEER_TOPK = 16
EPS = 1e-6
NEG = -1e30

LANES = 128
VMEM_LIMIT = 56 * 1024 * 1024

NT_DIMS = (((1,), (1,)), ((), ()))


def _cparams(n_axes):
    return pltpu.CompilerParams(dimension_semantics=("arbitrary",) * n_axes, vmem_limit_bytes=VMEM_LIMIT)


def _rms(x, g):
    return x * lax.rsqrt(jnp.mean(x * x, axis=-1, keepdims=True) + EPS) * g


def _gelu(x):
    return 0.5 * x * (1.0 + jnp.tanh(math.sqrt(2.0 / math.pi) * (x + 0.044715 * (x * x * x))))


def _dot(a, b):
    return jnp.dot(a, b, preferred_element_type=F32)


def _dot_nt(a, b):
    return lax.dot_general(a, b, NT_DIMS, preferred_element_type=F32)


def _mix_in(x, gmix, win, lng, lnb):
    xn = _rms(x, gmix).astype(BF16)
    h = _dot(xn, win)
    z = _gelu(h[:, :2 * MIX_A])
    u = z[:, :MIX_A]
    vs = []
    for g in range(A_GROUPS):
        seg = z[:, MIX_A + g * A_GW:MIX_A + (g + 1) * A_GW]
        mu = jnp.mean(seg, axis=-1, keepdims=True)
        cen = seg - mu
        var = jnp.mean(cen * cen, axis=-1, keepdims=True)
        vs.append(cen * lax.rsqrt(var + EPS) * lng[:, g * A_GW:(g + 1) * A_GW] + lnb[:, g * A_GW:(g + 1) * A_GW])
    v = jnp.concatenate(vs, axis=1)
    r0 = 2 * MIX_A
    q = h[:, r0:r0 + MIX_B]
    k = h[:, r0 + MIX_B:r0 + MIX_B + N_KV * HEAD_DIM]
    vb = h[:, r0 + MIX_B + N_KV * HEAD_DIM:]
    return u, v, q, k, vb


def _swa_softmax(s, mask, sk):
    s = jnp.where(mask, s, NEG)
    m = jnp.maximum(jnp.max(s, axis=-1, keepdims=True), sk)
    p = jnp.exp(s - m)
    return p / (jnp.sum(p, axis=-1, keepdims=True) + jnp.exp(sk - m))


def _mix_prompt_kernel(x_ref, gmix_ref, win_ref, lng_ref, lnb_ref, wm_ref, bs_ref, bias_ref, sink_ref,
                       ga_ref, gb_ref, woa_ref, wob_ref, h1_ref, k_ref, v_ref,
                       kprev, vprev, ya_scr, ob_scr, *, rows, tiles_per_seq):
    i = pl.program_id(0)
    x = x_ref[...]
    u, v, q, k, vb = _mix_in(x, gmix_ref[...], win_ref[...], lng_ref[...], lnb_ref[...])
    k_ref[...] = k
    v_ref[...] = vb
    first_tile = (i % tiles_per_seq) == 0

    @pl.when(first_tile)
    def _():
        kprev[...] = jnp.zeros_like(kprev)
        vprev[...] = jnp.zeros_like(vprev)

    nq = N_HEADS * WINDOW
    t_idx = lax.broadcasted_iota(jnp.int32, (nq, 2 * WINDOW), 0) % WINDOW
    kj = lax.broadcasted_iota(jnp.int32, (nq, 2 * WINDOW), 1)
    band = (kj > t_idx) & (kj <= t_idx + WINDOW)
    kj_min = jnp.where(first_tile, WINDOW, 0)
    band_first = band & (kj >= kj_min)
    lane = lax.broadcasted_iota(jnp.int32, (1, LANES), 1)
    lane_c = [lane < HEAD_DIM, lane >= HEAD_DIM]
    sk = sink_ref[...]
    bias = bias_ref[...]

    for j in range(rows // WINDOW):
        r = slice(j * WINDOW, (j + 1) * WINDOW)
        for g in range(A_GROUPS):
            c = slice(g * A_GW, (g + 1) * A_GW)
            s = _dot(wm_ref[g], v[r, c].astype(BF16)) + bs_ref[g]
            ya_scr[r, c] = u[r, c] * s
        k_prev = kprev[...] if j == 0 else k[(j - 1) * WINDOW:j * WINDOW]
        v_prev = vprev[...] if j == 0 else vb[(j - 1) * WINDOW:j * WINDOW]
        kcat = jnp.concatenate([k_prev, k[r]], axis=0).astype(BF16)
        vcat = jnp.concatenate([v_prev, vb[r]], axis=0)
        qg = jnp.concatenate([q[r, g * LANES:(g + 1) * LANES] for g in range(GQA)], axis=0)
        qs = jnp.concatenate([jnp.where(lane_c[c], qg, 0.0) for c in range(N_KV)], axis=0).astype(BF16)
        s = _dot_nt(qs, kcat) * (1.0 / math.sqrt(HEAD_DIM)) + bias
        p = _swa_softmax(s, band_first if j == 0 else band, sk).astype(BF16)
        half = GQA * WINDOW
        o = (_dot(p[:half], jnp.where(lane_c[0], vcat, 0.0).astype(BF16))
             + _dot(p[half:], jnp.where(lane_c[1], vcat, 0.0).astype(BF16)))
        for g in range(GQA):
            ob_scr[r, g * LANES:(g + 1) * LANES] = o[g * WINDOW:(g + 1) * WINDOW]

    kprev[...] = k[rows - WINDOW:]
    vprev[...] = vb[rows - WINDOW:]
    ya = _rms(ya_scr[...], ga_ref[...]).astype(BF16)
    ob = _rms(ob_scr[...], gb_ref[...]).astype(BF16)
    h1_ref[...] = x + _dot(ya, woa_ref[...]) + _dot(ob, wob_ref[...])


def _mix_sample_kernel(x_ref, ck_ref, cv_ref, gmix_ref, win_ref, lng_ref, lnb_ref, wm_ref, bs_ref, bias_ref,
                       sink_ref, ga_ref, gb_ref, woa_ref, wob_ref,
                       h1_ref, ko_ref, vo_ref, cvout_ref, kk_scr, vv_scr, *, nseq, seq_len):
    i = pl.program_id(0)
    wbuf = WINDOW

    @pl.when(i == 0)
    def _():
        kk_scr[...] = jnp.zeros_like(kk_scr)
        vv_scr[...] = jnp.zeros_like(vv_scr)

    x = x_ref[...]
    u, v, q, k, vb = _mix_in(x, gmix_ref[...], win_ref[...], lng_ref[...], lnb_ref[...])
    cvout_ref[...] = v
    ya_parts = []
    for g in range(A_GROUPS):
        c = slice(g * A_GW, (g + 1) * A_GW)
        s = _dot(wm_ref[g], v[:, c].astype(BF16)) + bs_ref[g]
        ya_parts.append(u[:, c] * s)
    ya = jnp.concatenate(ya_parts, axis=1)

    kk_scr[:, 0:wbuf, :] = ck_ref[...]
    vv_scr[:, 0:wbuf, :] = cv_ref[...]
    kk_scr[:, wbuf:wbuf + seq_len, :] = k.reshape(nseq, seq_len, LANES)
    vv_scr[:, wbuf:wbuf + seq_len, :] = vb.reshape(nseq, seq_len, LANES)
    ko_ref[...] = kk_scr[:, seq_len:wbuf + seq_len, :]
    vo_ref[...] = vv_scr[:, seq_len:wbuf + seq_len, :]

    nq = N_HEADS * seq_len
    l_idx = lax.broadcasted_iota(jnp.int32, (nq, 2 * WINDOW), 0) % seq_len
    kj = lax.broadcasted_iota(jnp.int32, (nq, 2 * WINDOW), 1)
    band = (kj > l_idx) & (kj <= l_idx + wbuf)
    lane = lax.broadcasted_iota(jnp.int32, (1, 1, LANES), 2)
    lane_c = [lane < HEAD_DIM, lane >= HEAD_DIM]

    q3 = q.reshape(nseq, seq_len, MIX_B)
    qg = jnp.concatenate([q3[:, :, g * LANES:(g + 1) * LANES] for g in range(GQA)], axis=1)
    qs = jnp.concatenate([jnp.where(lane_c[c], qg, 0.0) for c in range(N_KV)], axis=1).astype(BF16)
    kk = kk_scr[...].astype(BF16)
    vv = vv_scr[...]
    s = jnp.einsum('nqe,nke->nqk', qs, kk, preferred_element_type=F32) * (1.0 / math.sqrt(HEAD_DIM))
    s = s + bias_ref[...][None]
    p = _swa_softmax(s, band[None], sink_ref[...][None]).astype(BF16)
    half = GQA * seq_len
    o = (jnp.einsum('nqk,nke->nqe', p[:, :half], jnp.where(lane_c[0], vv, 0.0).astype(BF16),
                    preferred_element_type=F32)
         + jnp.einsum('nqk,nke->nqe', p[:, half:], jnp.where(lane_c[1], vv, 0.0).astype(BF16),
                      preferred_element_type=F32))
    ob3 = jnp.concatenate([o[:, g * seq_len:(g + 1) * seq_len, :] for g in range(GQA)], axis=2)
    ob = ob3.reshape(nseq * seq_len, MIX_B)
    ya_n = _rms(ya, ga_ref[...]).astype(BF16)
    ob_n = _rms(ob, gb_ref[...]).astype(BF16)
    h1_ref[...] = x + _dot(ya_n, woa_ref[...]) + _dot(ob_n, wob_ref[...])


def _mem_kv_kernel(mem_ref, g_ref, wk_ref, wv_ref, mk_ref, mv_ref):
    mn = _rms(mem_ref[0], g_ref[...]).astype(BF16)
    mk_ref[0] = _dot(mn, wk_ref[...])
    mv_ref[0] = _dot(mn, wv_ref[...])


def _mem_tail(h1, o, wmo_ref, gffn_ref, wq_ref, h2_ref, xn_ref, qp_ref):
    h2 = h1 + _dot(o.astype(BF16), wmo_ref[...])
    h2_ref[...] = h2
    xn = _rms(h2, gffn_ref[...]).astype(BF16)
    xn_ref[...] = xn
    qp = _dot(xn, wq_ref[...]).astype(BF16)
    for hp in range(2 * PEER_HEADS):
        qp_ref[hp] = qp[:, hp * PEER_HALF:(hp + 1) * PEER_HALF]


def _mem_prompt_kernel(h1_ref, mk_ref, mv_ref, gmem_ref, wmq_ref, wmo_ref, gffn_ref, wq_ref,
                       h2_ref, xn_ref, qp_ref):
    h1 = h1_ref[...]
    q = _dot(_rms(h1, gmem_ref[...]).astype(BF16), wmq_ref[...]).astype(BF16)
    mk = mk_ref[0].astype(BF16)
    mv = mv_ref[0].astype(BF16)
    outs = []
    for hh in range(MEM_HEADS):
        c = slice(hh * MEM_HEAD_DIM, (hh + 1) * MEM_HEAD_DIM)
        s = _dot_nt(q[:, c], mk[:, c]) * (1.0 / math.sqrt(MEM_HEAD_DIM))
        m = jnp.max(s, axis=-1, keepdims=True)
        p = jnp.exp(s - m)
        p = (p / jnp.sum(p, axis=-1, keepdims=True)).astype(BF16)
        outs.append(_dot(p, mv[:, c]))
    o = jnp.concatenate(outs, axis=1)
    _mem_tail(h1, o, wmo_ref, gffn_ref, wq_ref, h2_ref, xn_ref, qp_ref)


def _mem_sample_kernel(h1_ref, ck_ref, cv_ref, gmem_ref, wmq_ref, wmo_ref, gffn_ref, wq_ref,
                       h2_ref, xn_ref, qp_ref, *, nseq, seq_len):
    h1 = h1_ref[...]
    q = _dot(_rms(h1, gmem_ref[...]).astype(BF16), wmq_ref[...])
    q3 = q.reshape(nseq, seq_len, MEM_INNER)
    lane = lax.broadcasted_iota(jnp.int32, (1, 1, MEM_INNER), 2) // MEM_HEAD_DIM
    qe = jnp.concatenate([jnp.where(lane == hh, q3, 0.0) for hh in range(MEM_HEADS)], axis=1).astype(BF16)
    ck = ck_ref[...].astype(BF16)
    cv = cv_ref[...].astype(BF16)
    s = jnp.einsum('nqe,nke->nqk', qe, ck, preferred_element_type=F32) * (1.0 / math.sqrt(MEM_HEAD_DIM))
    m = jnp.max(s, axis=-1, keepdims=True)
    p = jnp.exp(s - m)
    p = (p / jnp.sum(p, axis=-1, keepdims=True)).astype(BF16)
    of = jnp.einsum('nqk,nke->nqe', p, cv, preferred_element_type=F32)
    o3 = jnp.where(lane == 0, of[:, 0:seq_len], 0.0)
    for hh in range(1, MEM_HEADS):
        o3 = o3 + jnp.where(lane == hh, of[:, hh * seq_len:(hh + 1) * seq_len], 0.0)
    o = o3.reshape(nseq * seq_len, MEM_INNER)
    _mem_tail(h1, o, wmo_ref, gffn_ref, wq_ref, h2_ref, xn_ref, qp_ref)


def _top_values(s, k):
    vals = []
    cur = s
    for r in range(k):
        m = jnp.max(cur, axis=0, keepdims=True)
        vals.append(m)
        if r + 1 < k:
            cur = jnp.where(cur == m, -jnp.inf, cur)
    return vals


def _peer_prep_kernel(qp_ref, keys_ref, s2_ref, e2_ref, th_ref, e1_ref, *, lane_tiles):
    def per_head(h, carry):
        for lt in range(lane_tiles):
            lanes = slice(lt * LANES, (lt + 1) * LANES)
            q1 = qp_ref[2 * h, lanes, :]
            q2 = qp_ref[2 * h + 1, lanes, :]
            s1 = _dot_nt(keys_ref[h, 0], q1)
            s2 = _dot_nt(keys_ref[h, 1], q2)
            a = _top_values(s1, PEER_TOPK)
            b = _top_values(s2, PEER_TOPK)
            a_lo = jnp.concatenate(a[:8], axis=0)
            a_hi = jnp.concatenate(a[8:], axis=0)
            b_lo = jnp.concatenate(b[:8], axis=0)
            b_hi = jnp.concatenate(b[8:], axis=0)
            cand = jnp.concatenate([a_lo + b[0], a_lo + b[1], a_lo + b[2], a_hi + b[0],
                                    a[0] + b_lo, a[0] + b_hi, a[1] + b_lo, a[2] + b_lo, a[3] + b_lo], axis=0)
            tops = _top_values(cand, PEER_TOPK)
            tau = tops[-1]
            z = jnp.ones_like(tau)
            for t in tops[1:]:
                z = z + jnp.exp(t - tops[0])
            th = jnp.full_like(s1, jnp.inf)
            for j in range(PEER_TOPK):
                th = jnp.where(s1 + b[j] >= tau, b[j], th)
            s2_ref[h, :, lanes] = s2
            e2_ref[h, :, lanes] = jnp.exp(s2 - b[0]) * (1.0 / z)
            th_ref[h, :, lanes] = th
            e1_ref[h, :, lanes] = jnp.exp(s1 - a[0])
        return carry

    lax.fori_loop(0, PEER_HEADS, per_head, 0)


def _peer_ffn_kernel(xn_ref, h2_ref, eu_ref, evt_ref, s2_ref, e2_ref, th_ref, e1_ref, gfin_ref,
                     y_ref, acc_ref, p_scr, *, tokens, i1_per_step):
    e = pl.program_id(1)

    @pl.when(e == 0)
    def _():
        acc_ref[...] = jnp.zeros_like(acc_ref)

    act = _gelu(_dot_nt(eu_ref[...], xn_ref[...]))
    i1_0 = pl.multiple_of(e * i1_per_step, i1_per_step)
    th_slab = [th_ref[h, pl.ds(i1_0, i1_per_step), :] for h in range(PEER_HEADS)]
    e1_slab = [e1_ref[h, pl.ds(i1_0, i1_per_step), :] for h in range(PEER_HEADS)]
    for lt in range(tokens // LANES):
        lanes = slice(lt * LANES, (lt + 1) * LANES)
        for j in range(i1_per_step):
            w = None
            for h in range(PEER_HEADS):
                sel = jnp.where(s2_ref[h, :, lanes] >= th_slab[h][j:j + 1, lanes], e2_ref[h, :, lanes], 0.0)
                term = sel * e1_slab[h][j:j + 1, lanes]
                w = term if w is None else w + term
            rows = slice(j * PEER_KEYS, (j + 1) * PEER_KEYS)
            p_scr[rows, lanes] = (w * act[rows, lanes]).astype(BF16)
    acc_ref[...] += _dot(evt_ref[...], p_scr[...])

    @pl.when(e == pl.num_programs(1) - 1)
    def _():
        y_ref[...] = _rms(h2_ref[...] + acc_ref[...].T, gfin_ref[...])


def _full(shape):
    return pl.BlockSpec(shape, lambda *_: (0,) * len(shape))


def _t5_bucket(d):
    n = jnp.maximum(d, 0)
    nf = jnp.maximum(n, 1).astype(F32)
    large = REL_MAX_EXACT + (jnp.log(nf / REL_MAX_EXACT) / math.log(REL_MAX_DIST / REL_MAX_EXACT)
                             * (REL_BUCKETS - REL_MAX_EXACT)).astype(jnp.int32)
    large = jnp.minimum(large, REL_BUCKETS - 1)
    return jnp.where(n < REL_MAX_EXACT, n, large)


def _bias_table(rel_bias, n_q):
    d = (jnp.arange(n_q)[:, None] + WINDOW) - jnp.arange(2 * WINDOW)[None, :]
    b = rel_bias[_t5_bucket(d)]
    return jnp.moveaxis(b, -1, 0).reshape(N_HEADS * n_q, 2 * WINDOW).astype(F32)


def _head_perm():
    g, c, d = jnp.meshgrid(jnp.arange(GQA), jnp.arange(N_KV), jnp.arange(HEAD_DIM), indexing='ij')
    return ((c * GQA + g) * HEAD_DIM + d).reshape(-1)


def _mix_weights(norm_mix_g, w_in, ln_v_g, ln_v_b, norm_a_out_g, norm_b_out_g, w_out):
    perm = _head_perm()
    q0 = 2 * MIX_A
    win = jnp.concatenate([w_in[:, :q0], w_in[:, q0:q0 + MIX_B][:, perm], w_in[:, q0 + MIX_B:]], axis=1).astype(BF16)
    woa = w_out[:MIX_A].astype(BF16)
    wob = w_out[MIX_A:][perm].astype(BF16)
    gb = norm_b_out_g[perm][None]
    return (norm_mix_g[None], win, ln_v_g[None], ln_v_b[None], norm_a_out_g[None], gb, woa, wob)


def _mix_prompt(x, wts, wm, bs, bias, sink, *, rows, seq):
    gmix, win, lng, lnb, ga, gb, woa, wob = wts
    t = x.shape[0]
    row_spec = lambda w: pl.BlockSpec((rows, w), lambda i: (i, 0))
    kern = functools.partial(_mix_prompt_kernel, rows=rows, tiles_per_seq=seq // rows)
    return pl.pallas_call(
        kern,
        grid=(t // rows,),
        in_specs=[row_spec(D_MODEL), _full(gmix.shape), _full(win.shape), _full(lng.shape), _full(lnb.shape),
                  _full(wm.shape), _full(bs.shape), _full(bias.shape), _full(sink.shape),
                  _full(ga.shape), _full(gb.shape), _full(woa.shape), _full(wob.shape)],
        out_specs=[row_spec(D_MODEL), row_spec(LANES), row_spec(LANES)],
        out_shape=[jax.ShapeDtypeStruct((t, D_MODEL), F32), jax.ShapeDtypeStruct((t, LANES), F32),
                   jax.ShapeDtypeStruct((t, LANES), F32)],
        scratch_shapes=[pltpu.VMEM((WINDOW, LANES), F32), pltpu.VMEM((WINDOW, LANES), F32),
                        pltpu.VMEM((rows, MIX_A), F32), pltpu.VMEM((rows, MIX_B), F32)],
        compiler_params=_cparams(1),
        name="mix_prompt",
    )(x, gmix, win, lng, lnb, wm, bs, bias, sink, ga, gb, woa, wob)


def _mix_sample(x, ck, cv, wts, wm, bs, bias, sink, *, nseq, seq_len):
    gmix, win, lng, lnb, ga, gb, woa, wob = wts
    t = x.shape[0]
    rows = nseq * seq_len
    row_spec = lambda w: pl.BlockSpec((rows, w), lambda i: (i, 0))
    seq_spec = pl.BlockSpec((nseq, WINDOW, LANES), lambda i: (i, 0, 0))
    kern = functools.partial(_mix_sample_kernel, nseq=nseq, seq_len=seq_len)
    return pl.pallas_call(
        kern,
        grid=(t // rows,),
        in_specs=[row_spec(D_MODEL), seq_spec, seq_spec, _full(gmix.shape), _full(win.shape), _full(lng.shape),
                  _full(lnb.shape), _full(wm.shape), _full(bs.shape), _full(bias.shape), _full(sink.shape),
                  _full(ga.shape), _full(gb.shape), _full(woa.shape), _full(wob.shape)],
        out_specs=[row_spec(D_MODEL), seq_spec, seq_spec, row_spec(MIX_A)],
        out_shape=[jax.ShapeDtypeStruct((t, D_MODEL), F32),
                   jax.ShapeDtypeStruct(ck.shape, F32), jax.ShapeDtypeStruct(cv.shape, F32),
                   jax.ShapeDtypeStruct((t, MIX_A), F32)],
        scratch_shapes=[pltpu.VMEM((nseq, 2 * WINDOW, LANES), F32), pltpu.VMEM((nseq, 2 * WINDOW, LANES), F32)],
        compiler_params=_cparams(1),
        name="mix_sample",
    )(x, ck, cv, gmix, win, lng, lnb, wm, bs, bias, sink, ga, gb, woa, wob)


def _mem_kv(mem, g, wk, wv):
    b, m, d = mem.shape
    return pl.pallas_call(
        _mem_kv_kernel,
        grid=(b,),
        in_specs=[pl.BlockSpec((1, m, d), lambda i: (i, 0, 0)), _full(g.shape), _full(wk.shape), _full(wv.shape)],
        out_specs=[pl.BlockSpec((1, m, MEM_INNER), lambda i: (i, 0, 0))] * 2,
        out_shape=[jax.ShapeDtypeStruct((b, m, MEM_INNER), F32)] * 2,
        compiler_params=_cparams(1),
        name="mem_kv",
    )(mem, g, wk, wv)


def _mem_outs(t, rows):
    row_spec = lambda w: pl.BlockSpec((rows, w), lambda i: (i, 0))
    specs = [row_spec(D_MODEL), row_spec(D_MODEL),
             pl.BlockSpec((2 * PEER_HEADS, rows, PEER_HALF), lambda i: (0, i, 0))]
    shapes = [jax.ShapeDtypeStruct((t, D_MODEL), F32), jax.ShapeDtypeStruct((t, D_MODEL), BF16),
              jax.ShapeDtypeStruct((2 * PEER_HEADS, t, PEER_HALF), BF16)]
    return specs, shapes


def _mem_prompt(h1, mk, mv, mwts, *, rows, seq):
    t = h1.shape[0]
    tiles_per_seq = seq // rows
    out_specs, out_shapes = _mem_outs(t, rows)
    kv_spec = pl.BlockSpec((1, N_MEM, MEM_INNER), lambda i: (i // tiles_per_seq, 0, 0))
    return pl.pallas_call(
        _mem_prompt_kernel,
        grid=(t // rows,),
        in_specs=[pl.BlockSpec((rows, D_MODEL), lambda i: (i, 0)), kv_spec, kv_spec] + [_full(w.shape) for w in mwts],
        out_specs=out_specs,
        out_shape=out_shapes,
        compiler_params=_cparams(1),
        name="mem_prompt",
    )(h1, mk, mv, *mwts)


def _mem_sample(h1, ck, cv, mwts, *, nseq, seq_len):
    t = h1.shape[0]
    rows = nseq * seq_len
    out_specs, out_shapes = _mem_outs(t, rows)
    kv_spec = pl.BlockSpec((nseq, N_MEM, MEM_INNER), lambda i: (i, 0, 0))
    kern = functools.partial(_mem_sample_kernel, nseq=nseq, seq_len=seq_len)
    return pl.pallas_call(
        kern,
        grid=(t // rows,),
        in_specs=[pl.BlockSpec((rows, D_MODEL), lambda i: (i, 0)), kv_spec, kv_spec] + [_full(w.shape) for w in mwts],
        out_specs=out_specs,
        out_shape=out_shapes,
        compiler_params=_cparams(1),
        name="mem_sample",
    )(h1, ck, cv, *mwts)


def _peer_prep(qp, keys, *, tokens):
    t = qp.shape[1]
    out_spec = pl.BlockSpec((PEER_HEADS, PEER_KEYS, tokens), lambda i: (0, 0, i))
    kern = functools.partial(_peer_prep_kernel, lane_tiles=tokens // LANES)
    return pl.pallas_call(
        kern,
        grid=(t // tokens,),
        in_specs=[pl.BlockSpec((2 * PEER_HEADS, tokens, PEER_HALF), lambda i: (0, i, 0)), _full(keys.shape)],
        out_specs=[out_spec] * 4,
        out_shape=[jax.ShapeDtypeStruct((PEER_HEADS, PEER_KEYS, t), F32)] * 4,
        compiler_params=_cparams(1),
        name="peer_prep",
    )(qp, keys)


def _peer_ffn(xn, h2, eu, evt, prep, gfin, *, tokens, i1_per_step):
    t = xn.shape[0]
    experts = i1_per_step * PEER_KEYS
    tok_spec = pl.BlockSpec((tokens, D_MODEL), lambda i, e: (i, 0))
    prep_spec = pl.BlockSpec((PEER_HEADS, PEER_KEYS, tokens), lambda i, e: (0, 0, i))
    kern = functools.partial(_peer_ffn_kernel, tokens=tokens, i1_per_step=i1_per_step)
    return pl.pallas_call(
        kern,
        grid=(t // tokens, PEER_N // experts),
        in_specs=[tok_spec, tok_spec,
                  pl.BlockSpec((experts, D_MODEL), lambda i, e: (e, 0)),
                  pl.BlockSpec((D_MODEL, experts), lambda i, e: (0, e)),
                  prep_spec, prep_spec, prep_spec, prep_spec,
                  pl.BlockSpec(gfin.shape, lambda i, e: (0, 0))],
        out_specs=tok_spec,
        out_shape=jax.ShapeDtypeStruct((t, D_MODEL), F32),
        scratch_shapes=[pltpu.VMEM((D_MODEL, tokens), F32), pltpu.VMEM((experts, tokens), BF16)],
        compiler_params=_cparams(2),
        name="peer_ffn",
    )(xn, h2, eu, evt, *prep, gfin)


def kernel(x_prompt, x_sample, mem_prompt, cache_swa_k, cache_swa_v, cache_mem_k, cache_mem_v, norm_mix_g, w_in, ln_v_g, ln_v_b, spatial_w, spatial_b, attn_sinks, rel_bias, norm_a_out_g, norm_b_out_g, w_out, norm_mem_g, norm_memsrc_g, w_mq, w_mk, w_mv, w_mo, norm_ffn_g, peer_wq, peer_keys, peer_u, peer_v, norm_final_g):
    batch, seq, _ = x_prompt.shape
    nsamp, dec_len, _ = x_sample.shape
    l = 0

    wts = _mix_weights(norm_mix_g[l], w_in[l], ln_v_g[l], ln_v_b[l],
                       norm_a_out_g[l], norm_b_out_g[l], w_out[l])
    tril = jnp.tril(jnp.ones((CHUNK, CHUNK), F32))
    wm = spatial_w[l] * tril
    wm_p = wm.astype(BF16)
    bs_p = spatial_b[l][:, :, None]
    seqs_per_chunk = CHUNK // dec_len
    eye = jnp.eye(seqs_per_chunk, dtype=F32)
    wm_s = jnp.einsum('ab,gts->gatbs', eye, wm[:, :dec_len, :dec_len]).reshape(A_GROUPS, CHUNK, CHUNK).astype(BF16)
    bs_s = jnp.tile(spatial_b[l][:, :dec_len], (1, seqs_per_chunk))[:, :, None]
    bias_p = _bias_table(rel_bias, WINDOW)
    bias_s = _bias_table(rel_bias, dec_len)
    sink_p = jnp.repeat(attn_sinks[l], WINDOW)[:, None]
    sink_s = jnp.repeat(attn_sinks[l], dec_len)[:, None]
    mwts = (norm_mem_g[l][None], w_mq[l].astype(BF16), w_mo[l].astype(BF16), norm_ffn_g[l][None],
            peer_wq[l].astype(BF16))
    keys = peer_keys[l].astype(BF16)
    eu = peer_u[l].astype(BF16)
    evt = peer_v[l].T.astype(BF16)
    gfin = norm_final_g[None]

    xp = x_prompt.reshape(batch * seq, D_MODEL)
    h1p, kp, vp = _mix_prompt(xp, wts, wm_p, bs_p, bias_p, sink_p, rows=512, seq=seq)
    mk, mv = _mem_kv(mem_prompt, norm_memsrc_g[l][None], w_mk[l].astype(BF16), w_mv[l].astype(BF16))
    h2p, xnp_, qpp = _mem_prompt(h1p, mk, mv, mwts, rows=512, seq=seq)
    prep_p = _peer_prep(qpp, keys, tokens=256)
    yp = _peer_ffn(xnp_, h2p, eu, evt, prep_p, gfin, tokens=512, i1_per_step=8)

    xs = x_sample.reshape(nsamp * dec_len, D_MODEL)
    ck = cache_swa_k[l].reshape(nsamp, WINDOW, LANES)
    cv = cache_swa_v[l].reshape(nsamp, WINDOW, LANES)
    h1s, ks, vs, chunk_v = _mix_sample(xs, ck, cv, wts, wm_s, bs_s, bias_s, sink_s,
                                       nseq=seqs_per_chunk, seq_len=dec_len)
    cmk = cache_mem_k[l].reshape(nsamp, N_MEM, MEM_INNER)
    cmv = cache_mem_v[l].reshape(nsamp, N_MEM, MEM_INNER)
    h2s, xns, qps = _mem_sample(h1s, cmk, cmv, mwts, nseq=8, seq_len=dec_len)
    prep_s = _peer_prep(qps, keys, tokens=256)
    ys = _peer_ffn(xns, h2s, eu, evt, prep_s, gfin, tokens=512, i1_per_step=8)

    kp4 = kp.reshape(batch, seq, N_KV, HEAD_DIM)[:, -WINDOW:]
    vp4 = vp.reshape(batch, seq, N_KV, HEAD_DIM)[:, -WINDOW:]
    return (yp.reshape(batch, seq, D_MODEL),
            ys.reshape(nsamp, dec_len, D_MODEL),
            kp4[None], vp4[None],
            mk.reshape(batch, N_MEM, MEM_HEADS, MEM_HEAD_DIM)[None],
            mv.reshape(batch, N_MEM, MEM_HEADS, MEM_HEAD_DIM)[None],
            ks.reshape(nsamp, WINDOW, N_KV, HEAD_DIM)[None],
            vs.reshape(nsamp, WINDOW, N_KV, HEAD_DIM)[None],
            chunk_v.reshape(nsamp, dec_len, A_GROUPS, A_GW)[None])
```

```python
import functools
import math

import jax
import jax.numpy as jnp
from jax import lax
from jax.experimental import pallas as pl
from jax.experimental.pallas import tpu as pltpu

F32 = jnp.float32
BF16 = jnp.bfloat16

D_MODEL = 1024
MIX_A = 512
A_GROUPS = 4
A_GW = 128
CHUNK = 128
N_HEADS = 8
N_KV = 2
HEAD_DIM = 64
GQA = 4
MIX_B = 512
WINDOW = 128
IN_COLS = 1792
REL_BUCKETS = 32
REL_MAX_EXACT = 16
REL_MAX_DIST = 128
N_MEM = 256
MEM_HEADS = 4
MEM_HEAD_DIM = 128
MEM_INNER = 512
PEER_HEADS = 8
PEER_KEYS = 128
PEER_N = PEER_KEYS * PEER_KEYS
PEER_HALF = 128
PEER_TOPK = 16
EPS = 1e-6
NEG = -1e30

LANES = 128
SUBLANES = 8
VMEM_LIMIT = 56 * 1024 * 1024

NT_DIMS = (((1,), (1,)), ((), ()))


def _cparams(n_axes):
    return pltpu.CompilerParams(dimension_semantics=("arbitrary",) * n_axes, vmem_limit_bytes=VMEM_LIMIT)


def _rms(x, g):
    return x * lax.rsqrt(jnp.mean(x * x, axis=-1, keepdims=True) + EPS) * g


def _gelu(x):
    return 0.5 * x * (1.0 + jnp.tanh(math.sqrt(2.0 / math.pi) * (x + 0.044715 * (x * x * x))))


def _dot(a, b):
    return jnp.dot(a, b, preferred_element_type=F32)


def _dot_nt(a, b):
    return lax.dot_general(a, b, NT_DIMS, preferred_element_type=F32)


def _mix_in(x, gmix, win, lng, lnb):
    xn = _rms(x, gmix).astype(BF16)
    h = _dot(xn, win)
    z = _gelu(h[:, :2 * MIX_A])
    u = z[:, :MIX_A]
    vs = []
    for g in range(A_GROUPS):
        seg = z[:, MIX_A + g * A_GW:MIX_A + (g + 1) * A_GW]
        mu = jnp.mean(seg, axis=-1, keepdims=True)
        cen = seg - mu
        var = jnp.mean(cen * cen, axis=-1, keepdims=True)
        vs.append(cen * lax.rsqrt(var + EPS) * lng[:, g * A_GW:(g + 1) * A_GW] + lnb[:, g * A_GW:(g + 1) * A_GW])
    v = jnp.concatenate(vs, axis=1)
    r0 = 2 * MIX_A
    q = h[:, r0:r0 + MIX_B]
    k = h[:, r0 + MIX_B:r0 + MIX_B + N_KV * HEAD_DIM]
    vb = h[:, r0 + MIX_B + N_KV * HEAD_DIM:]
    return u, v, q, k, vb


def _swa_softmax(s, mask, sk):
    s = jnp.where(mask, s, NEG)
    m = jnp.maximum(jnp.max(s, axis=-1, keepdims=True), sk)
    p = jnp.exp(s - m)
    return p / (jnp.sum(p, axis=-1, keepdims=True) + jnp.exp(sk - m))


def _mix_prompt_kernel(x_ref, gmix_ref, win_ref, lng_ref, lnb_ref, wm_ref, bs_ref, bias_ref, sink_ref,
                       ga_ref, gb_ref, woa_ref, wob_ref, h1_ref, k_ref, v_ref,
                       kprev, vprev, ya_scr, ob_scr, *, rows, tiles_per_seq):
    i = pl.program_id(0)
    x = x_ref[...]
    u, v, q, k, vb = _mix_in(x, gmix_ref[...], win_ref[...], lng_ref[...], lnb_ref[...])
    k_ref[...] = k
    v_ref[...] = vb
    first_tile = (i % tiles_per_seq) == 0

    @pl.when(first_tile)
    def _():
        kprev[...] = jnp.zeros_like(kprev)
        vprev[...] = jnp.zeros_like(vprev)

    nq = N_HEADS * WINDOW
    t_idx = lax.broadcasted_iota(jnp.int32, (nq, 2 * WINDOW), 0) % WINDOW
    kj = lax.broadcasted_iota(jnp.int32, (nq, 2 * WINDOW), 1)
    band = (kj > t_idx) & (kj <= t_idx + WINDOW)
    kj_min = jnp.where(first_tile, WINDOW, 0)
    band_first = band & (kj >= kj_min)
    lane = lax.broadcasted_iota(jnp.int32, (1, LANES), 1)
    lane_c = [lane < HEAD_DIM, lane >= HEAD_DIM]
    sk = sink_ref[...]
    bias = bias_ref[...]

    for j in range(rows // WINDOW):
        r = slice(j * WINDOW, (j + 1) * WINDOW)
        for g in range(A_GROUPS):
            c = slice(g * A_GW, (g + 1) * A_GW)
            s = _dot(wm_ref[g], v[r, c].astype(BF16)) + bs_ref[g]
            ya_scr[r, c] = u[r, c] * s
        k_prev = kprev[...] if j == 0 else k[(j - 1) * WINDOW:j * WINDOW]
        v_prev = vprev[...] if j == 0 else vb[(j - 1) * WINDOW:j * WINDOW]
        kcat = jnp.concatenate([k_prev, k[r]], axis=0).astype(BF16)
        vcat = jnp.concatenate([v_prev, vb[r]], axis=0)
        qg = jnp.concatenate([q[r, g * LANES:(g + 1) * LANES] for g in range(GQA)], axis=0)
        qs = jnp.concatenate([jnp.where(lane_c[c], qg, 0.0) for c in range(N_KV)], axis=0).astype(BF16)
        s = _dot_nt(qs, kcat) * (1.0 / math.sqrt(HEAD_DIM)) + bias
        p = _swa_softmax(s, band_first if j == 0 else band, sk).astype(BF16)
        half = GQA * WINDOW
        o = (_dot(p[:half], jnp.where(lane_c[0], vcat, 0.0).astype(BF16))
             + _dot(p[half:], jnp.where(lane_c[1], vcat, 0.0).astype(BF16)))
        for g in range(GQA):
            ob_scr[r, g * LANES:(g + 1) * LANES] = o[g * WINDOW:(g + 1) * WINDOW]

    kprev[...] = k[rows - WINDOW:]
    vprev[...] = vb[rows - WINDOW:]
    ya = _rms(ya_scr[...], ga_ref[...]).astype(BF16)
    ob = _rms(ob_scr[...], gb_ref[...]).astype(BF16)
    h1_ref[...] = x + _dot(ya, woa_ref[...]) + _dot(ob, wob_ref[...])


def _mix_sample_kernel(x_ref, ck_ref, cv_ref, gmix_ref, win_ref, lng_ref, lnb_ref, wm_ref, bs_ref, bias_ref,
                       sink_ref, ga_ref, gb_ref, woa_ref, wob_ref,
                       h1_ref, ko_ref, vo_ref, cvout_ref, kk_scr, vv_scr, *, nseq, seq_len):
    i = pl.program_id(0)
    wbuf = WINDOW

    @pl.when(i == 0)
    def _():
        kk_scr[...] = jnp.zeros_like(kk_scr)
        vv_scr[...] = jnp.zeros_like(vv_scr)

    x = x_ref[...]
    u, v, q, k, vb = _mix_in(x, gmix_ref[...], win_ref[...], lng_ref[...], lnb_ref[...])
    cvout_ref[...] = v
    ya_parts = []
    for g in range(A_GROUPS):
        c = slice(g * A_GW, (g + 1) * A_GW)
        s = _dot(wm_ref[g], v[:, c].astype(BF16)) + bs_ref[g]
        ya_parts.append(u[:, c] * s)
    ya = jnp.concatenate(ya_parts, axis=1)

    kk_scr[:, 0:wbuf, :] = ck_ref[...]
    vv_scr[:, 0:wbuf, :] = cv_ref[...]
    kk_scr[:, wbuf:wbuf + seq_len, :] = k.reshape(nseq, seq_len, LANES)
    vv_scr[:, wbuf:wbuf + seq_len, :] = vb.reshape(nseq, seq_len, LANES)
    ko_ref[...] = kk_scr[:, seq_len:wbuf + seq_len, :]
    vo_ref[...] = vv_scr[:, seq_len:wbuf + seq_len, :]

    nq = N_HEADS * seq_len
    l_idx = lax.broadcasted_iota(jnp.int32, (nq, 2 * WINDOW), 0) % seq_len
    kj = lax.broadcasted_iota(jnp.int32, (nq, 2 * WINDOW), 1)
    band = (kj > l_idx) & (kj <= l_idx + wbuf)
    lane = lax.broadcasted_iota(jnp.int32, (1, 1, LANES), 2)
    lane_c = [lane < HEAD_DIM, lane >= HEAD_DIM]

    q3 = q.reshape(nseq, seq_len, MIX_B)
    qg = jnp.concatenate([q3[:, :, g * LANES:(g + 1) * LANES] for g in range(GQA)], axis=1)
    qs = jnp.concatenate([jnp.where(lane_c[c], qg, 0.0) for c in range(N_KV)], axis=1).astype(BF16)
    kk = kk_scr[...].astype(BF16)
    vv = vv_scr[...]
    s = jnp.einsum('nqe,nke->nqk', qs, kk, preferred_element_type=F32) * (1.0 / math.sqrt(HEAD_DIM))
    s = s + bias_ref[...][None]
    p = _swa_softmax(s, band[None], sink_ref[...][None]).astype(BF16)
    half = GQA * seq_len
    o = (jnp.einsum('nqk,nke->nqe', p[:, :half], jnp.where(lane_c[0], vv, 0.0).astype(BF16),
                    preferred_element_type=F32)
         + jnp.einsum('nqk,nke->nqe', p[:, half:], jnp.where(lane_c[1], vv, 0.0).astype(BF16),
                      preferred_element_type=F32))
    ob3 = jnp.concatenate([o[:, g * seq_len:(g + 1) * seq_len, :] for g in range(GQA)], axis=2)
    ob = ob3.reshape(nseq * seq_len, MIX_B)
    ya_n = _rms(ya, ga_ref[...]).astype(BF16)
    ob_n = _rms(ob, gb_ref[...]).astype(BF16)
    h1_ref[...] = x + _dot(ya_n, woa_ref[...]) + _dot(ob_n, wob_ref[...])


def _mem_kv_kernel(mem_ref, g_ref, wk_ref, wv_ref, mk_ref, mv_ref):
    mn = _rms(mem_ref[0], g_ref[...]).astype(BF16)
    mk_ref[0] = _dot(mn, wk_ref[...])
    mv_ref[0] = _dot(mn, wv_ref[...])


def _mem_tail(h1, o, wmo_ref, gffn_ref, wq_ref, h2_ref, xn_ref, qp_ref):
    h2 = h1 + _dot(o.astype(BF16), wmo_ref[...])
    h2_ref[...] = h2
    xn = _rms(h2, gffn_ref[...]).astype(BF16)
    xn_ref[...] = xn
    qp = _dot(xn, wq_ref[...]).astype(BF16)
    for hp in range(2 * PEER_HEADS):
        qp_ref[hp] = qp[:, hp * PEER_HALF:(hp + 1) * PEER_HALF]


def _mem_prompt_kernel(h1_ref, mk_ref, mv_ref, gmem_ref, wmq_ref, wmo_ref, gffn_ref, wq_ref,
                       h2_ref, xn_ref, qp_ref):
    h1 = h1_ref[...]
    q = _dot(_rms(h1, gmem_ref[...]).astype(BF16), wmq_ref[...]).astype(BF16)
    mk = mk_ref[0].astype(BF16)
    mv = mv_ref[0].astype(BF16)
    outs = []
    for hh in range(MEM_HEADS):
        c = slice(hh * MEM_HEAD_DIM, (hh + 1) * MEM_HEAD_DIM)
        s = _dot_nt(q[:, c], mk[:, c]) * (1.0 / math.sqrt(MEM_HEAD_DIM))
        m = jnp.max(s, axis=-1, keepdims=True)
        p = jnp.exp(s - m)
        p = (p / jnp.sum(p, axis=-1, keepdims=True)).astype(BF16)
        outs.append(_dot(p, mv[:, c]))
    o = jnp.concatenate(outs, axis=1)
    _mem_tail(h1, o, wmo_ref, gffn_ref, wq_ref, h2_ref, xn_ref, qp_ref)


def _mem_sample_kernel(h1_ref, ck_ref, cv_ref, gmem_ref, wmq_ref, wmo_ref, gffn_ref, wq_ref,
                       h2_ref, xn_ref, qp_ref, *, nseq, seq_len):
    h1 = h1_ref[...]
    q = _dot(_rms(h1, gmem_ref[...]).astype(BF16), wmq_ref[...])
    q3 = q.reshape(nseq, seq_len, MEM_INNER)
    lane = lax.broadcasted_iota(jnp.int32, (1, 1, MEM_INNER), 2) // MEM_HEAD_DIM
    qe = jnp.concatenate([jnp.where(lane == hh, q3, 0.0) for hh in range(MEM_HEADS)], axis=1).astype(BF16)
    ck = ck_ref[...].astype(BF16)
    cv = cv_ref[...].astype(BF16)
    s = jnp.einsum('nqe,nke->nqk', qe, ck, preferred_element_type=F32) * (1.0 / math.sqrt(MEM_HEAD_DIM))
    m = jnp.max(s, axis=-1, keepdims=True)
    p = jnp.exp(s - m)
    p = (p / jnp.sum(p, axis=-1, keepdims=True)).astype(BF16)
    of = jnp.einsum('nqk,nke->nqe', p, cv, preferred_element_type=F32)
    o3 = jnp.where(lane == 0, of[:, 0:seq_len], 0.0)
    for hh in range(1, MEM_HEADS):
        o3 = o3 + jnp.where(lane == hh, of[:, hh * seq_len:(hh + 1) * seq_len], 0.0)
    o = o3.reshape(nseq * seq_len, MEM_INNER)
    _mem_tail(h1, o, wmo_ref, gffn_ref, wq_ref, h2_ref, xn_ref, qp_ref)


def _top_values(s, k, with_rank=False):
    vals = []
    cur = s
    rank = jnp.full_like(s, float(k)) if with_rank else None
    for r in range(k):
        m = jnp.max(cur, axis=0, keepdims=True)
        vals.append(m)
        hit = cur == m
        if with_rank:
            rank = jnp.where(hit, float(r), rank)
        if r + 1 < k:
            cur = jnp.where(hit, -jnp.inf, cur)
    return (vals, rank) if with_rank else vals


def _peer_prep_kernel(qp_ref, keys_ref, r2_ref, e2_ref, cnt_ref, e1_ref, *, lane_tiles):
    def per_head(h, carry):
        for lt in range(lane_tiles):
            lanes = slice(lt * LANES, (lt + 1) * LANES)
            q1 = qp_ref[2 * h, lanes, :]
            q2 = qp_ref[2 * h + 1, lanes, :]
            s1 = _dot_nt(keys_ref[h, 0], q1)
            s2 = _dot_nt(keys_ref[h, 1], q2)
            a = _top_values(s1, PEER_TOPK)
            b, rank2 = _top_values(s2, PEER_TOPK, with_rank=True)
            a_lo = jnp.concatenate(a[:8], axis=0)
            a_hi = jnp.concatenate(a[8:], axis=0)
            b_lo = jnp.concatenate(b[:8], axis=0)
            b_hi = jnp.concatenate(b[8:], axis=0)
            cand = jnp.concatenate([a_lo + b[0], a_lo + b[1], a_lo + b[2], a_hi + b[0],
                                    a[0] + b_lo, a[0] + b_hi, a[1] + b_lo, a[2] + b_lo, a[3] + b_lo], axis=0)
            tops = _top_values(cand, PEER_TOPK)
            tau = tops[-1]
            z = jnp.ones_like(tau)
            for t in tops[1:]:
                z = z + jnp.exp(t - tops[0])
            cnt = jnp.zeros_like(s1)
            for j in range(PEER_TOPK):
                cnt = jnp.where(s1 + b[j] >= tau, float(j + 1), cnt)
            r2_ref[h, :, lanes] = rank2.astype(BF16)
            e2_ref[h, :, lanes] = (jnp.exp(s2 - b[0]) * (1.0 / z)).astype(BF16)
            cnt_ref[h, :, lanes] = cnt
            e1_ref[h, :, lanes] = jnp.exp(s1 - a[0])
        return carry

    lax.fori_loop(0, PEER_HEADS, per_head, 0)


def _peer_ffn_kernel(xn_ref, h2_ref, eu_ref, evt_ref, r2_ref, e2_ref, cnt_ref, e1_ref, gfin_ref,
                     y_ref, acc_ref, p_scr, *, tokens, i1_per_step):
    e = pl.program_id(1)

    @pl.when(e == 0)
    def _():
        acc_ref[...] = jnp.zeros_like(acc_ref)

    x = _dot_nt(eu_ref[...], xn_ref[...])
    c0 = math.sqrt(2.0 / math.pi)
    hx = 0.5 * x
    act = hx + hx * jnp.tanh(x * ((0.044715 * c0) * (x * x) + c0))
    i1_0 = pl.multiple_of(e * i1_per_step, i1_per_step)
    cnt_slab = [cnt_ref[h, pl.ds(i1_0, i1_per_step), :] for h in range(PEER_HEADS)]
    e1_slab = [e1_ref[h, pl.ds(i1_0, i1_per_step), :] for h in range(PEER_HEADS)]
    sub, width = SUBLANES, 2 * LANES
    for lt in range(tokens // width):
        lanes = slice(lt * width, (lt + 1) * width)
        for j in range(i1_per_step):
            w = None
            for h in range(PEER_HEADS):
                cnt = jnp.broadcast_to(cnt_slab[h][j:j + 1, lanes], (sub, width)).astype(BF16)
                e1 = jnp.broadcast_to(e1_slab[h][j:j + 1, lanes], (sub, width)).astype(BF16)
                cnt = jnp.concatenate([cnt] * (PEER_KEYS // sub), axis=0)
                e1 = jnp.concatenate([e1] * (PEER_KEYS // sub), axis=0)
                e2 = e2_ref[h, :, lanes]
                term = jnp.where(r2_ref[h, :, lanes] < cnt, e2, jnp.zeros_like(e2)) * e1
                w = term if w is None else w + term
            rows = slice(j * PEER_KEYS, (j + 1) * PEER_KEYS)
            p_scr[rows, lanes] = w * act[rows, lanes].astype(BF16)
    acc_ref[...] += _dot(evt_ref[...], p_scr[...])

    @pl.when(e == pl.num_programs(1) - 1)
    def _():
        y_ref[...] = _rms(h2_ref[...] + acc_ref[...].T, gfin_ref[...])


def _full(shape):
    return pl.BlockSpec(shape, lambda *_: (0,) * len(shape))


def _t5_bucket(d):
    n = jnp.maximum(d, 0)
    nf = jnp.maximum(n, 1).astype(F32)
    large = REL_MAX_EXACT + (jnp.log(nf / REL_MAX_EXACT) / math.log(REL_MAX_DIST / REL_MAX_EXACT)
                             * (REL_BUCKETS - REL_MAX_EXACT)).astype(jnp.int32)
    large = jnp.minimum(large, REL_BUCKETS - 1)
    return jnp.where(n < REL_MAX_EXACT, n, large)


def _bias_table(rel_bias, n_q):
    d = (jnp.arange(n_q)[:, None] + WINDOW) - jnp.arange(2 * WINDOW)[None, :]
    b = rel_bias[_t5_bucket(d)]
    return jnp.moveaxis(b, -1, 0).reshape(N_HEADS * n_q, 2 * WINDOW).astype(F32)


def _head_perm():
    g, c, d = jnp.meshgrid(jnp.arange(GQA), jnp.arange(N_KV), jnp.arange(HEAD_DIM), indexing='ij')
    return ((c * GQA + g) * HEAD_DIM + d).reshape(-1)


def _mix_weights(norm_mix_g, w_in, ln_v_g, ln_v_b, norm_a_out_g, norm_b_out_g, w_out):
    perm = _head_perm()
    q0 = 2 * MIX_A
    win = jnp.concatenate([w_in[:, :q0], w_in[:, q0:q0 + MIX_B][:, perm], w_in[:, q0 + MIX_B:]], axis=1).astype(BF16)
    woa = w_out[:MIX_A].astype(BF16)
    wob = w_out[MIX_A:][perm].astype(BF16)
    gb = norm_b_out_g[perm][None]
    return (norm_mix_g[None], win, ln_v_g[None], ln_v_b[None], norm_a_out_g[None], gb, woa, wob)


def _mix_prompt(x, wts, wm, bs, bias, sink, *, rows, seq):
    gmix, win, lng, lnb, ga, gb, woa, wob = wts
    t = x.shape[0]
    row_spec = lambda w: pl.BlockSpec((rows, w), lambda i: (i, 0))
    kern = functools.partial(_mix_prompt_kernel, rows=rows, tiles_per_seq=seq // rows)
    return pl.pallas_call(
        kern,
        grid=(t // rows,),
        in_specs=[row_spec(D_MODEL), _full(gmix.shape), _full(win.shape), _full(lng.shape), _full(lnb.shape),
                  _full(wm.shape), _full(bs.shape), _full(bias.shape), _full(sink.shape),
                  _full(ga.shape), _full(gb.shape), _full(woa.shape), _full(wob.shape)],
        out_specs=[row_spec(D_MODEL), row_spec(LANES), row_spec(LANES)],
        out_shape=[jax.ShapeDtypeStruct((t, D_MODEL), F32), jax.ShapeDtypeStruct((t, LANES), F32),
                   jax.ShapeDtypeStruct((t, LANES), F32)],
        scratch_shapes=[pltpu.VMEM((WINDOW, LANES), F32), pltpu.VMEM((WINDOW, LANES), F32),
                        pltpu.VMEM((rows, MIX_A), F32), pltpu.VMEM((rows, MIX_B), F32)],
        compiler_params=_cparams(1),
        name="mix_prompt",
    )(x, gmix, win, lng, lnb, wm, bs, bias, sink, ga, gb, woa, wob)


def _mix_sample(x, ck, cv, wts, wm, bs, bias, sink, *, nseq, seq_len):
    gmix, win, lng, lnb, ga, gb, woa, wob = wts
    t = x.shape[0]
    rows = nseq * seq_len
    row_spec = lambda w: pl.BlockSpec((rows, w), lambda i: (i, 0))
    seq_spec = pl.BlockSpec((nseq, WINDOW, LANES), lambda i: (i, 0, 0))
    kern = functools.partial(_mix_sample_kernel, nseq=nseq, seq_len=seq_len)
    return pl.pallas_call(
        kern,
        grid=(t // rows,),
        in_specs=[row_spec(D_MODEL), seq_spec, seq_spec, _full(gmix.shape), _full(win.shape), _full(lng.shape),
                  _full(lnb.shape), _full(wm.shape), _full(bs.shape), _full(bias.shape), _full(sink.shape),
                  _full(ga.shape), _full(gb.shape), _full(woa.shape), _full(wob.shape)],
        out_specs=[row_spec(D_MODEL), seq_spec, seq_spec, row_spec(MIX_A)],
        out_shape=[jax.ShapeDtypeStruct((t, D_MODEL), F32),
                   jax.ShapeDtypeStruct(ck.shape, F32), jax.ShapeDtypeStruct(cv.shape, F32),
                   jax.ShapeDtypeStruct((t, MIX_A), F32)],
        scratch_shapes=[pltpu.VMEM((nseq, 2 * WINDOW, LANES), F32), pltpu.VMEM((nseq, 2 * WINDOW, LANES), F32)],
        compiler_params=_cparams(1),
        name="mix_sample",
    )(x, ck, cv, gmix, win, lng, lnb, wm, bs, bias, sink, ga, gb, woa, wob)


def _mem_kv(mem, g, wk, wv):
    b, m, d = mem.shape
    return pl.pallas_call(
        _mem_kv_kernel,
        grid=(b,),
        in_specs=[pl.BlockSpec((1, m, d), lambda i: (i, 0, 0)), _full(g.shape), _full(wk.shape), _full(wv.shape)],
        out_specs=[pl.BlockSpec((1, m, MEM_INNER), lambda i: (i, 0, 0))] * 2,
        out_shape=[jax.ShapeDtypeStruct((b, m, MEM_INNER), F32)] * 2,
        compiler_params=_cparams(1),
        name="mem_kv",
    )(mem, g, wk, wv)


def _mem_outs(t, rows):
    row_spec = lambda w: pl.BlockSpec((rows, w), lambda i: (i, 0))
    specs = [row_spec(D_MODEL), row_spec(D_MODEL),
             pl.BlockSpec((2 * PEER_HEADS, rows, PEER_HALF), lambda i: (0, i, 0))]
    shapes = [jax.ShapeDtypeStruct((t, D_MODEL), F32), jax.ShapeDtypeStruct((t, D_MODEL), BF16),
              jax.ShapeDtypeStruct((2 * PEER_HEADS, t, PEER_HALF), BF16)]
    return specs, shapes


def _mem_prompt(h1, mk, mv, mwts, *, rows, seq):
    t = h1.shape[0]
    tiles_per_seq = seq // rows
    out_specs, out_shapes = _mem_outs(t, rows)
    kv_spec = pl.BlockSpec((1, N_MEM, MEM_INNER), lambda i: (i // tiles_per_seq, 0, 0))
    return pl.pallas_call(
        _mem_prompt_kernel,
        grid=(t // rows,),
        in_specs=[pl.BlockSpec((rows, D_MODEL), lambda i: (i, 0)), kv_spec, kv_spec] + [_full(w.shape) for w in mwts],
        out_specs=out_specs,
        out_shape=out_shapes,
        compiler_params=_cparams(1),
        name="mem_prompt",
    )(h1, mk, mv, *mwts)


def _mem_sample(h1, ck, cv, mwts, *, nseq, seq_len):
    t = h1.shape[0]
    rows = nseq * seq_len
    out_specs, out_shapes = _mem_outs(t, rows)
    kv_spec = pl.BlockSpec((nseq, N_MEM, MEM_INNER), lambda i: (i, 0, 0))
    kern = functools.partial(_mem_sample_kernel, nseq=nseq, seq_len=seq_len)
    return pl.pallas_call(
        kern,
        grid=(t // rows,),
        in_specs=[pl.BlockSpec((rows, D_MODEL), lambda i: (i, 0)), kv_spec, kv_spec] + [_full(w.shape) for w in mwts],
        out_specs=out_specs,
        out_shape=out_shapes,
        compiler_params=_cparams(1),
        name="mem_sample",
    )(h1, ck, cv, *mwts)


def _peer_prep(qp, keys, *, tokens):
    t = qp.shape[1]
    out_spec = pl.BlockSpec((PEER_HEADS, PEER_KEYS, tokens), lambda i: (0, 0, i))
    kern = functools.partial(_peer_prep_kernel, lane_tiles=tokens // LANES)
    return pl.pallas_call(
        kern,
        grid=(t // tokens,),
        in_specs=[pl.BlockSpec((2 * PEER_HEADS, tokens, PEER_HALF), lambda i: (0, i, 0)), _full(keys.shape)],
        out_specs=[out_spec] * 4,
        out_shape=[jax.ShapeDtypeStruct((PEER_HEADS, PEER_KEYS, t), dt) for dt in (BF16, BF16, F32, F32)],
        compiler_params=_cparams(1),
        name="peer_prep",
    )(qp, keys)


def _peer_ffn(xn, h2, eu, evt, prep, gfin, *, tokens, i1_per_step):
    t = xn.shape[0]
    experts = i1_per_step * PEER_KEYS
    tok_spec = pl.BlockSpec((tokens, D_MODEL), lambda i, e: (i, 0))
    prep_spec = pl.BlockSpec((PEER_HEADS, PEER_KEYS, tokens), lambda i, e: (0, 0, i))
    kern = functools.partial(_peer_ffn_kernel, tokens=tokens, i1_per_step=i1_per_step)
    return pl.pallas_call(
        kern,
        grid=(t // tokens, PEER_N // experts),
        in_specs=[tok_spec, tok_spec,
                  pl.BlockSpec((experts, D_MODEL), lambda i, e: (e, 0)),
                  pl.BlockSpec((D_MODEL, experts), lambda i, e: (0, e)),
                  prep_spec, prep_spec, prep_spec, prep_spec,
                  pl.BlockSpec(gfin.shape, lambda i, e: (0, 0))],
        out_specs=tok_spec,
        out_shape=jax.ShapeDtypeStruct((t, D_MODEL), F32),
        scratch_shapes=[pltpu.VMEM((D_MODEL, tokens), F32), pltpu.VMEM((experts, tokens), BF16)],
        compiler_params=_cparams(2),
        name="peer_ffn",
    )(xn, h2, eu, evt, *prep, gfin)


def kernel(x_prompt, x_sample, mem_prompt, cache_swa_k, cache_swa_v, cache_mem_k, cache_mem_v, norm_mix_g, w_in, ln_v_g, ln_v_b, spatial_w, spatial_b, attn_sinks, rel_bias, norm_a_out_g, norm_b_out_g, w_out, norm_mem_g, norm_memsrc_g, w_mq, w_mk, w_mv, w_mo, norm_ffn_g, peer_wq, peer_keys, peer_u, peer_v, norm_final_g):
    batch, seq, _ = x_prompt.shape
    nsamp, dec_len, _ = x_sample.shape
    l = 0

    wts = _mix_weights(norm_mix_g[l], w_in[l], ln_v_g[l], ln_v_b[l],
                       norm_a_out_g[l], norm_b_out_g[l], w_out[l])
    tril = jnp.tril(jnp.ones((CHUNK, CHUNK), F32))
    wm = spatial_w[l] * tril
    wm_p = wm.astype(BF16)
    bs_p = spatial_b[l][:, :, None]
    seqs_per_chunk = CHUNK // dec_len
    eye = jnp.eye(seqs_per_chunk, dtype=F32)
    wm_s = jnp.einsum('ab,gts->gatbs', eye, wm[:, :dec_len, :dec_len]).reshape(A_GROUPS, CHUNK, CHUNK).astype(BF16)
    bs_s = jnp.tile(spatial_b[l][:, :dec_len], (1, seqs_per_chunk))[:, :, None]
    bias_p = _bias_table(rel_bias, WINDOW)
    bias_s = _bias_table(rel_bias, dec_len)
    sink_p = jnp.repeat(attn_sinks[l], WINDOW)[:, None]
    sink_s = jnp.repeat(attn_sinks[l], dec_len)[:, None]
    mwts = (norm_mem_g[l][None], w_mq[l].astype(BF16), w_mo[l].astype(BF16), norm_ffn_g[l][None],
            peer_wq[l].astype(BF16))
    keys = peer_keys[l].astype(BF16)
    eu = peer_u[l].astype(BF16)
    evt = peer_v[l].T.astype(BF16)
    gfin = norm_final_g[None]

    xp = x_prompt.reshape(batch * seq, D_MODEL)
    h1p, kp, vp = _mix_prompt(xp, wts, wm_p, bs_p, bias_p, sink_p, rows=512, seq=seq)
    mk, mv = _mem_kv(mem_prompt, norm_memsrc_g[l][None], w_mk[l].astype(BF16), w_mv[l].astype(BF16))
    h2p, xnp_, qpp = _mem_prompt(h1p, mk, mv, mwts, rows=512, seq=seq)
    prep_p = _peer_prep(qpp, keys, tokens=256)
    yp = _peer_ffn(xnp_, h2p, eu, evt, prep_p, gfin, tokens=512, i1_per_step=8)

    xs = x_sample.reshape(nsamp * dec_len, D_MODEL)
    ck = cache_swa_k[l].reshape(nsamp, WINDOW, LANES)
    cv = cache_swa_v[l].reshape(nsamp, WINDOW, LANES)
    h1s, ks, vs, chunk_v = _mix_sample(xs, ck, cv, wts, wm_s, bs_s, bias_s, sink_s,
                                       nseq=seqs_per_chunk, seq_len=dec_len)
    cmk = cache_mem_k[l].reshape(nsamp, N_MEM, MEM_INNER)
    cmv = cache_mem_v[l].reshape(nsamp, N_MEM, MEM_INNER)
    h2s, xns, qps = _mem_sample(h1s, cmk, cmv, mwts, nseq=8, seq_len=dec_len)
    prep_s = _peer_prep(qps, keys, tokens=256)
    ys = _peer_ffn(xns, h2s, eu, evt, prep_s, gfin, tokens=512, i1_per_step=8)

    kp4 = kp.reshape(batch, seq, N_KV, HEAD_DIM)[:, -WINDOW:]
    vp4 = vp.reshape(batch, seq, N_KV, HEAD_DIM)[:, -WINDOW:]
    return (yp.reshape(batch, seq, D_MODEL),
            ys.reshape(nsamp, dec_len, D_MODEL),
            kp4[None], vp4[None],
            mk.reshape(batch, N_MEM, MEM_HEADS, MEM_HEAD_DIM)[None],
            mv.reshape(batch, N_MEM, MEM_HEADS, MEM_HEAD_DIM)[None],
            ks.reshape(nsamp, WINDOW, N_KV, HEAD_DIM)[None],
            vs.reshape(nsamp, WINDOW, N_KV, HEAD_DIM)[None],
            chunk_v.reshape(nsamp, dec_len, A_GROUPS, A_GW)[None])
```

```python
import functools
import math

import jax
import jax.numpy as jnp
import numpy as np
from jax import lax
from jax.experimental import pallas as pl
from jax.experimental.pallas import tpu as pltpu

F32 = jnp.float32
BF16 = jnp.bfloat16

D_MODEL = 1024
MIX_A = 512
A_GROUPS = 4
A_GW = 128
CHUNK = 128
N_HEADS = 8
N_KV = 2
HEAD_DIM = 64
GQA = 4
MIX_B = 512
WINDOW = 128
IN_COLS = 1792
REL_BUCKETS = 32
REL_MAX_EXACT = 16
REL_MAX_DIST = 128
N_MEM = 256
MEM_HEADS = 4
MEM_HEAD_DIM = 128
MEM_INNER = 512
PEER_HEADS = 8
PEER_KEYS = 128
PEER_N = PEER_KEYS * PEER_KEYS
PEER_HALF = 128
PEER_TOPK = 16
EPS = 1e-6
NEG = -1e30

LANES = 128
SUBLANES = 8
VMEM_LIMIT = 56 * 1024 * 1024

NT_DIMS = (((1,), (1,)), ((), ()))


def _cparams(n_axes):
    return pltpu.CompilerParams(dimension_semantics=("arbitrary",) * n_axes, vmem_limit_bytes=VMEM_LIMIT)


def _rms(x, g):
    return x * lax.rsqrt(jnp.mean(x * x, axis=-1, keepdims=True) + EPS) * g


def _gelu(x):
    return 0.5 * x * (1.0 + jnp.tanh(math.sqrt(2.0 / math.pi) * (x + 0.044715 * (x * x * x))))


def _dot(a, b):
    return jnp.dot(a, b, preferred_element_type=F32)


def _dot_nt(a, b):
    return lax.dot_general(a, b, NT_DIMS, preferred_element_type=F32)


def _mix_in(x, gmix, win, lng, lnb):
    xn = _rms(x, gmix).astype(BF16)
    h = _dot(xn, win)
    z = _gelu(h[:, :2 * MIX_A])
    u = z[:, :MIX_A]
    vs = []
    for g in range(A_GROUPS):
        seg = z[:, MIX_A + g * A_GW:MIX_A + (g + 1) * A_GW]
        mu = jnp.mean(seg, axis=-1, keepdims=True)
        cen = seg - mu
        var = jnp.mean(cen * cen, axis=-1, keepdims=True)
        vs.append(cen * lax.rsqrt(var + EPS) * lng[:, g * A_GW:(g + 1) * A_GW] + lnb[:, g * A_GW:(g + 1) * A_GW])
    v = jnp.concatenate(vs, axis=1)
    r0 = 2 * MIX_A
    q = h[:, r0:r0 + MIX_B]
    k = h[:, r0 + MIX_B:r0 + MIX_B + N_KV * HEAD_DIM]
    vb = h[:, r0 + MIX_B + N_KV * HEAD_DIM:]
    return u, v, q, k, vb


def _swa_softmax(s, mask, sk):
    s = jnp.where(mask, s, NEG)
    m = jnp.maximum(jnp.max(s, axis=-1, keepdims=True), sk)
    p = jnp.exp(s - m)
    return p / (jnp.sum(p, axis=-1, keepdims=True) + jnp.exp(sk - m))


def _fill_rel_bias(bias_scr, bkt_ref, relb_ref):
    bkt = bkt_ref[...]
    n_q = bkt.shape[0]
    for h in range(N_HEADS):
        acc = jnp.zeros(bkt.shape, F32)
        for b in range(REL_BUCKETS):
            acc = jnp.where(bkt == b, relb_ref[b:b + 1, h:h + 1], acc)
        bias_scr[h * n_q:(h + 1) * n_q, :] = acc


def _mix_prompt_kernel(x_ref, gmix_ref, win_ref, lng_ref, lnb_ref, wm_ref, bs_ref, bkt_ref, relb_ref, sink_ref,
                       ga_ref, gb_ref, woa_ref, wob_ref, h1_ref, k_ref, v_ref,
                       kprev, vprev, ya_scr, ob_scr, bias_ref, *, rows, tiles_per_seq):
    i = pl.program_id(0)

    @pl.when(i == 0)
    def _():
        _fill_rel_bias(bias_ref, bkt_ref, relb_ref)

    x = x_ref[...]
    u, v, q, k, vb = _mix_in(x, gmix_ref[...], win_ref[...], lng_ref[...], lnb_ref[...])
    k_ref[...] = k
    v_ref[...] = vb
    first_tile = (i % tiles_per_seq) == 0

    @pl.when(first_tile)
    def _():
        kprev[...] = jnp.zeros_like(kprev)
        vprev[...] = jnp.zeros_like(vprev)

    nq = N_HEADS * WINDOW
    t_idx = lax.broadcasted_iota(jnp.int32, (nq, 2 * WINDOW), 0) % WINDOW
    kj = lax.broadcasted_iota(jnp.int32, (nq, 2 * WINDOW), 1)
    band = (kj > t_idx) & (kj <= t_idx + WINDOW)
    kj_min = jnp.where(first_tile, WINDOW, 0)
    band_first = band & (kj >= kj_min)
    lane = lax.broadcasted_iota(jnp.int32, (1, LANES), 1)
    lane_c = [lane < HEAD_DIM, lane >= HEAD_DIM]
    sk = sink_ref[...]
    bias = bias_ref[...]

    for j in range(rows // WINDOW):
        r = slice(j * WINDOW, (j + 1) * WINDOW)
        for g in range(A_GROUPS):
            c = slice(g * A_GW, (g + 1) * A_GW)
            s = _dot(wm_ref[g], v[r, c].astype(BF16)) + bs_ref[g]
            ya_scr[r, c] = u[r, c] * s
        k_prev = kprev[...] if j == 0 else k[(j - 1) * WINDOW:j * WINDOW]
        v_prev = vprev[...] if j == 0 else vb[(j - 1) * WINDOW:j * WINDOW]
        kcat = jnp.concatenate([k_prev, k[r]], axis=0).astype(BF16)
        vcat = jnp.concatenate([v_prev, vb[r]], axis=0)
        qg = jnp.concatenate([q[r, g * LANES:(g + 1) * LANES] for g in range(GQA)], axis=0)
        qs = jnp.concatenate([jnp.where(lane_c[c], qg, 0.0) for c in range(N_KV)], axis=0).astype(BF16)
        s = _dot_nt(qs, kcat) * (1.0 / math.sqrt(HEAD_DIM)) + bias
        p = _swa_softmax(s, band_first if j == 0 else band, sk).astype(BF16)
        half = GQA * WINDOW
        o = (_dot(p[:half], jnp.where(lane_c[0], vcat, 0.0).astype(BF16))
             + _dot(p[half:], jnp.where(lane_c[1], vcat, 0.0).astype(BF16)))
        for g in range(GQA):
            ob_scr[r, g * LANES:(g + 1) * LANES] = o[g * WINDOW:(g + 1) * WINDOW]

    kprev[...] = k[rows - WINDOW:]
    vprev[...] = vb[rows - WINDOW:]
    ya = _rms(ya_scr[...], ga_ref[...]).astype(BF16)
    ob = _rms(ob_scr[...], gb_ref[...]).astype(BF16)
    h1_ref[...] = x + _dot(ya, woa_ref[...]) + _dot(ob, wob_ref[...])


def _mix_sample_kernel(x_ref, ck_ref, cv_ref, gmix_ref, win_ref, lng_ref, lnb_ref, wm_ref, bs_ref, bkt_ref,
                       relb_ref, sink_ref, ga_ref, gb_ref, woa_ref, wob_ref,
                       h1_ref, ko_ref, vo_ref, cvout_ref, kk_scr, vv_scr, bias_ref, *, nseq, seq_len):
    i = pl.program_id(0)
    wbuf = WINDOW

    @pl.when(i == 0)
    def _():
        kk_scr[...] = jnp.zeros_like(kk_scr)
        vv_scr[...] = jnp.zeros_like(vv_scr)
        _fill_rel_bias(bias_ref, bkt_ref, relb_ref)

    x = x_ref[...]
    u, v, q, k, vb = _mix_in(x, gmix_ref[...], win_ref[...], lng_ref[...], lnb_ref[...])
    cvout_ref[...] = v
    ya_parts = []
    for g in range(A_GROUPS):
        c = slice(g * A_GW, (g + 1) * A_GW)
        s = _dot(wm_ref[g], v[:, c].astype(BF16)) + bs_ref[g]
        ya_parts.append(u[:, c] * s)
    ya = jnp.concatenate(ya_parts, axis=1)

    kk_scr[:, 0:wbuf, :] = ck_ref[...]
    vv_scr[:, 0:wbuf, :] = cv_ref[...]
    kk_scr[:, wbuf:wbuf + seq_len, :] = k.reshape(nseq, seq_len, LANES)
    vv_scr[:, wbuf:wbuf + seq_len, :] = vb.reshape(nseq, seq_len, LANES)
    ko_ref[...] = kk_scr[:, seq_len:wbuf + seq_len, :]
    vo_ref[...] = vv_scr[:, seq_len:wbuf + seq_len, :]

    nq = N_HEADS * seq_len
    l_idx = lax.broadcasted_iota(jnp.int32, (nq, 2 * WINDOW), 0) % seq_len
    kj = lax.broadcasted_iota(jnp.int32, (nq, 2 * WINDOW), 1)
    band = (kj > l_idx) & (kj <= l_idx + wbuf)
    lane = lax.broadcasted_iota(jnp.int32, (1, 1, LANES), 2)
    lane_c = [lane < HEAD_DIM, lane >= HEAD_DIM]

    q3 = q.reshape(nseq, seq_len, MIX_B)
    qg = jnp.concatenate([q3[:, :, g * LANES:(g + 1) * LANES] for g in range(GQA)], axis=1)
    qs = jnp.concatenate([jnp.where(lane_c[c], qg, 0.0) for c in range(N_KV)], axis=1).astype(BF16)
    kk = kk_scr[...].astype(BF16)
    vv = vv_scr[...]
    s = jnp.einsum('nqe,nke->nqk', qs, kk, preferred_element_type=F32) * (1.0 / math.sqrt(HEAD_DIM))
    s = s + bias_ref[...][None]
    p = _swa_softmax(s, band[None], sink_ref[...][None]).astype(BF16)
    half = GQA * seq_len
    o = (jnp.einsum('nqk,nke->nqe', p[:, :half], jnp.where(lane_c[0], vv, 0.0).astype(BF16),
                    preferred_element_type=F32)
         + jnp.einsum('nqk,nke->nqe', p[:, half:], jnp.where(lane_c[1], vv, 0.0).astype(BF16),
                      preferred_element_type=F32))
    ob3 = jnp.concatenate([o[:, g * seq_len:(g + 1) * seq_len, :] for g in range(GQA)], axis=2)
    ob = ob3.reshape(nseq * seq_len, MIX_B)
    ya_n = _rms(ya, ga_ref[...]).astype(BF16)
    ob_n = _rms(ob, gb_ref[...]).astype(BF16)
    h1_ref[...] = x + _dot(ya_n, woa_ref[...]) + _dot(ob_n, wob_ref[...])


def _mem_kv_kernel(mem_ref, g_ref, wk_ref, wv_ref, mk_ref, mv_ref):
    mn = _rms(mem_ref[0], g_ref[...]).astype(BF16)
    mk_ref[0] = _dot(mn, wk_ref[...])
    mv_ref[0] = _dot(mn, wv_ref[...])


def _mem_tail(h1, o, wmo_ref, gffn_ref, wq_ref, h2_ref, xn_ref, qp_ref):
    h2 = h1 + _dot(o.astype(BF16), wmo_ref[...])
    h2_ref[...] = h2
    xn = _rms(h2, gffn_ref[...]).astype(BF16)
    xn_ref[...] = xn
    qp = _dot(xn, wq_ref[...]).astype(BF16)
    for hp in range(2 * PEER_HEADS):
        qp_ref[hp] = qp[:, hp * PEER_HALF:(hp + 1) * PEER_HALF]


def _mem_prompt_kernel(h1_ref, mk_ref, mv_ref, gmem_ref, wmq_ref, wmo_ref, gffn_ref, wq_ref,
                       h2_ref, xn_ref, qp_ref):
    h1 = h1_ref[...]
    q = _dot(_rms(h1, gmem_ref[...]).astype(BF16), wmq_ref[...]).astype(BF16)
    mk = mk_ref[0].astype(BF16)
    mv = mv_ref[0].astype(BF16)
    outs = []
    for hh in range(MEM_HEADS):
        c = slice(hh * MEM_HEAD_DIM, (hh + 1) * MEM_HEAD_DIM)
        s = _dot_nt(q[:, c], mk[:, c]) * (1.0 / math.sqrt(MEM_HEAD_DIM))
        m = jnp.max(s, axis=-1, keepdims=True)
        p = jnp.exp(s - m)
        p = (p / jnp.sum(p, axis=-1, keepdims=True)).astype(BF16)
        outs.append(_dot(p, mv[:, c]))
    o = jnp.concatenate(outs, axis=1)
    _mem_tail(h1, o, wmo_ref, gffn_ref, wq_ref, h2_ref, xn_ref, qp_ref)


def _mem_sample_kernel(h1_ref, ck_ref, cv_ref, gmem_ref, wmq_ref, wmo_ref, gffn_ref, wq_ref,
                       h2_ref, xn_ref, qp_ref, *, nseq, seq_len):
    h1 = h1_ref[...]
    q = _dot(_rms(h1, gmem_ref[...]).astype(BF16), wmq_ref[...])
    q3 = q.reshape(nseq, seq_len, MEM_INNER)
    qs = jnp.concatenate([q3[:, :, hh * MEM_HEAD_DIM:(hh + 1) * MEM_HEAD_DIM] for hh in range(MEM_HEADS)],
                         axis=1).astype(BF16)
    ck = ck_ref[...].astype(BF16)
    cv = cv_ref[...].astype(BF16)
    s = jnp.einsum('nqd,nkd->nqk', qs, ck, preferred_element_type=F32) * (1.0 / math.sqrt(MEM_HEAD_DIM))
    nq, nk = MEM_HEADS * seq_len, N_MEM * MEM_HEADS
    row_head = lax.broadcasted_iota(jnp.int32, (nq, nk), 0) // seq_len
    col_head = lax.broadcasted_iota(jnp.int32, (nq, nk), 1) % MEM_HEADS
    s = jnp.where((row_head == col_head)[None], s, -jnp.inf)
    m = jnp.max(s, axis=-1, keepdims=True)
    p = jnp.exp(s - m)
    p = (p / jnp.sum(p, axis=-1, keepdims=True)).astype(BF16)
    of = jnp.einsum('nqk,nkd->nqd', p, cv, preferred_element_type=F32)
    o3 = jnp.concatenate([of[:, hh * seq_len:(hh + 1) * seq_len] for hh in range(MEM_HEADS)], axis=2)
    o = o3.reshape(nseq * seq_len, MEM_INNER)
    _mem_tail(h1, o, wmo_ref, gffn_ref, wq_ref, h2_ref, xn_ref, qp_ref)


def _sort_network(n):
    out, p = [], 1
    while p < n:
        k = p
        while k >= 1:
            for j in range(k % p, n - k, 2 * k):
                for i in range(min(k, n - j - k)):
                    if (i + j) // (2 * p) == (i + j + k) // (2 * p):
                        out.append((i + j, i + j + k))
            k //= 2
        p *= 2
    return out


_SORT16 = _sort_network(PEER_TOPK)


def _compare_exchange(x, i, j):
    hi, lo = jnp.maximum(x[i], x[j]), jnp.minimum(x[i], x[j])
    x[i], x[j] = hi, lo


def _top16_sorted(slabs):
    n = len(slabs)
    x = list(slabs)
    for i, j in _SORT16:
        if j < n:
            _compare_exchange(x, i, j)
    for shift in (4, 2, 1):
        y = []
        for i in range(PEER_TOPK):
            lo = x[i] if i < n else None
            k = PEER_TOPK - 1 - i
            hi = pltpu.roll(x[k], shift, axis=0) if k < n else None
            y.append(hi if lo is None else lo if hi is None else jnp.maximum(lo, hi))
        d = PEER_TOPK // 2
        while d >= 1:
            for i in range(PEER_TOPK):
                if i & d == 0:
                    _compare_exchange(y, i, i + d)
            d //= 2
        x, n = y, PEER_TOPK
    return x


def _step_count(slabs, thresholds, below):
    out = []
    for x in slabs:
        c = jnp.zeros_like(x)
        for j, t in enumerate(thresholds):
            c = jnp.where((x < t) if below else (x >= t), float(j + 1), c)
        out.append(c)
    return out


def _peer_prep_kernel(qp_ref, keys_ref, r2_ref, e2_ref, cnt_ref, e1_ref, *, lane_tiles):
    nslab = PEER_KEYS // SUBLANES

    def per_head(h, carry):
        for lt in range(lane_tiles):
            lanes = slice(lt * LANES, (lt + 1) * LANES)
            q1 = qp_ref[2 * h, lanes, :]
            q2 = qp_ref[2 * h + 1, lanes, :]
            s1 = _dot_nt(keys_ref[h, 0], q1)
            s2 = _dot_nt(keys_ref[h, 1], q2)
            s1s = [s1[r * SUBLANES:(r + 1) * SUBLANES] for r in range(nslab)]
            s2s = [s2[r * SUBLANES:(r + 1) * SUBLANES] for r in range(nslab)]
            a = _top16_sorted(s1s)
            b = _top16_sorted(s2s)
            sub = lax.broadcasted_iota(jnp.int32, a[0].shape, 0)

            def by_sublane(vals):
                out = vals[SUBLANES - 1]
                for r in range(SUBLANES - 2, -1, -1):
                    out = jnp.where(sub == r, vals[r], out)
                return out

            a_lo, a_hi = by_sublane(a[:8]), by_sublane(a[8:])
            b_lo, b_hi = by_sublane(b[:8]), by_sublane(b[8:])
            tail = lambda x: jnp.where(sub >= 3, x, -jnp.inf)
            cand = [a_lo + b[0], a_hi + b[0], a_lo + b[1], a_lo + b[2], tail(a[0] + b_lo), a[0] + b_hi,
                    tail(a[1] + b_lo), tail(a[2] + b_lo), tail(a[3] + b_lo)]
            tops = _top16_sorted(cand)
            tau = tops[-1]
            z = jnp.ones_like(tau)
            for t in tops[1:]:
                z = z + jnp.exp(t - tops[0])
            thr = []
            for j in range(PEER_TOPK):
                t = jnp.full_like(tau, jnp.inf)
                for i in range(PEER_TOPK // (j + 1)):
                    t = jnp.where(a[i] + b[j] >= tau, a[i], t)
                thr.append(t)
            cnt = _step_count(s1s, thr, below=False)
            rank2 = _step_count(s2s, b, below=True)
            half_inv_z = 0.5 / z
            r2_ref[h, :, lanes] = jnp.concatenate(rank2, axis=0).astype(BF16)
            e2_ref[h, :, lanes] = jnp.concatenate([jnp.exp(x - b[0]) * half_inv_z for x in s2s], axis=0).astype(BF16)
            cnt_ref[h, :, lanes] = jnp.concatenate(cnt, axis=0)
            e1_ref[h, :, lanes] = jnp.concatenate([jnp.exp(x - a[0]) for x in s1s], axis=0)
        return carry

    lax.fori_loop(0, PEER_HEADS, per_head, 0)


def _peer_ffn_kernel(xn_ref, h2_ref, eu_ref, evt_ref, r2_ref, e2_ref, cnt_ref, e1_ref, gfin_ref,
                     y_ref, acc_ref, p_scr, *, tokens, i1_per_step):
    e = pl.program_id(1)

    @pl.when(e == 0)
    def _():
        acc_ref[...] = jnp.zeros_like(acc_ref)

    x = _dot_nt(eu_ref[...], xn_ref[...])
    c0 = math.sqrt(2.0 / math.pi)
    act = x + x * jnp.tanh(x * ((0.044715 * c0) * (x * x) + c0))
    i1_0 = pl.multiple_of(e * i1_per_step, i1_per_step)
    cnt_slab = [cnt_ref[h, pl.ds(i1_0, i1_per_step), :] for h in range(PEER_HEADS)]
    e1_slab = [e1_ref[h, pl.ds(i1_0, i1_per_step), :] for h in range(PEER_HEADS)]
    sub, width = SUBLANES, 2 * LANES
    for lt in range(tokens // width):
        lanes = slice(lt * width, (lt + 1) * width)
        for j in range(i1_per_step):
            w = None
            for h in range(PEER_HEADS):
                cnt = jnp.broadcast_to(cnt_slab[h][j:j + 1, lanes], (sub, width)).astype(BF16)
                e1 = jnp.broadcast_to(e1_slab[h][j:j + 1, lanes], (sub, width)).astype(BF16)
                cnt = jnp.concatenate([cnt] * (PEER_KEYS // sub), axis=0)
                e1 = jnp.concatenate([e1] * (PEER_KEYS // sub), axis=0)
                e2 = e2_ref[h, :, lanes]
                term = jnp.where(r2_ref[h, :, lanes] < cnt, e2, jnp.zeros_like(e2)) * e1
                w = term if w is None else w + term
            rows = slice(j * PEER_KEYS, (j + 1) * PEER_KEYS)
            p_scr[rows, lanes] = w * act[rows, lanes].astype(BF16)
    acc_ref[...] += _dot(evt_ref[...], p_scr[...])

    @pl.when(e == pl.num_programs(1) - 1)
    def _():
        y_ref[...] = _rms(h2_ref[...] + acc_ref[...].T, gfin_ref[...])


def _full(shape):
    return pl.BlockSpec(shape, lambda *_: (0,) * len(shape))


def _t5_bucket(d):
    n = np.maximum(d, 0)
    nf = np.maximum(n, 1).astype(np.float32)
    scaled = (np.log(nf / np.float32(REL_MAX_EXACT)) / np.float32(math.log(REL_MAX_DIST / REL_MAX_EXACT))
              * np.float32(REL_BUCKETS - REL_MAX_EXACT))
    frac = np.abs(scaled - np.round(scaled))
    capped = scaled > REL_BUCKETS - REL_MAX_EXACT
    assert np.all((frac > 1e-3) | (frac < 1e-5) | capped), "bucket boundary too close to an integer distance"
    large = np.minimum(REL_MAX_EXACT + np.floor(scaled + 1e-4).astype(np.int32), REL_BUCKETS - 1)
    return np.where(n < REL_MAX_EXACT, n, large).astype(np.int32)


def _bucket_table(n_q):
    d = (np.arange(n_q)[:, None] + WINDOW) - np.arange(2 * WINDOW)[None, :]
    return jnp.asarray(_t5_bucket(d))


def _head_perm():
    g, c, d = jnp.meshgrid(jnp.arange(GQA), jnp.arange(N_KV), jnp.arange(HEAD_DIM), indexing='ij')
    return ((c * GQA + g) * HEAD_DIM + d).reshape(-1)


def _mix_weights(norm_mix_g, w_in, ln_v_g, ln_v_b, norm_a_out_g, norm_b_out_g, w_out):
    perm = _head_perm()
    q0 = 2 * MIX_A
    win = jnp.concatenate([w_in[:, :q0], w_in[:, q0:q0 + MIX_B][:, perm], w_in[:, q0 + MIX_B:]], axis=1).astype(BF16)
    woa = w_out[:MIX_A].astype(BF16)
    wob = w_out[MIX_A:][perm].astype(BF16)
    gb = norm_b_out_g[perm][None]
    return (norm_mix_g[None], win, ln_v_g[None], ln_v_b[None], norm_a_out_g[None], gb, woa, wob)


def _mix_prompt(x, wts, wm, bs, bkt, relb, sink, *, rows, seq):
    gmix, win, lng, lnb, ga, gb, woa, wob = wts
    t = x.shape[0]
    row_spec = lambda w: pl.BlockSpec((rows, w), lambda i: (i, 0))
    kern = functools.partial(_mix_prompt_kernel, rows=rows, tiles_per_seq=seq // rows)
    return pl.pallas_call(
        kern,
        grid=(t // rows,),
        in_specs=[row_spec(D_MODEL), _full(gmix.shape), _full(win.shape), _full(lng.shape), _full(lnb.shape),
                  _full(wm.shape), _full(bs.shape), _full(bkt.shape), _full(relb.shape), _full(sink.shape),
                  _full(ga.shape), _full(gb.shape), _full(woa.shape), _full(wob.shape)],
        out_specs=[row_spec(D_MODEL), row_spec(LANES), row_spec(LANES)],
        out_shape=[jax.ShapeDtypeStruct((t, D_MODEL), F32), jax.ShapeDtypeStruct((t, LANES), F32),
                   jax.ShapeDtypeStruct((t, LANES), F32)],
        scratch_shapes=[pltpu.VMEM((WINDOW, LANES), F32), pltpu.VMEM((WINDOW, LANES), F32),
                        pltpu.VMEM((rows, MIX_A), F32), pltpu.VMEM((rows, MIX_B), F32),
                        pltpu.VMEM((N_HEADS * WINDOW, 2 * WINDOW), F32)],
        compiler_params=_cparams(1),
        name="mix_prompt",
    )(x, gmix, win, lng, lnb, wm, bs, bkt, relb, sink, ga, gb, woa, wob)


def _mix_sample(x, ck, cv, wts, wm, bs, bkt, relb, sink, *, nseq, seq_len):
    gmix, win, lng, lnb, ga, gb, woa, wob = wts
    t = x.shape[0]
    rows = nseq * seq_len
    row_spec = lambda w: pl.BlockSpec((rows, w), lambda i: (i, 0))
    seq_spec = pl.BlockSpec((nseq, WINDOW, LANES), lambda i: (i, 0, 0))
    kern = functools.partial(_mix_sample_kernel, nseq=nseq, seq_len=seq_len)
    return pl.pallas_call(
        kern,
        grid=(t // rows,),
        in_specs=[row_spec(D_MODEL), seq_spec, seq_spec, _full(gmix.shape), _full(win.shape), _full(lng.shape),
                  _full(lnb.shape), _full(wm.shape), _full(bs.shape), _full(bkt.shape), _full(relb.shape),
                  _full(sink.shape),
                  _full(ga.shape), _full(gb.shape), _full(woa.shape), _full(wob.shape)],
        out_specs=[row_spec(D_MODEL), seq_spec, seq_spec, row_spec(MIX_A)],
        out_shape=[jax.ShapeDtypeStruct((t, D_MODEL), F32),
                   jax.ShapeDtypeStruct(ck.shape, F32), jax.ShapeDtypeStruct(cv.shape, F32),
                   jax.ShapeDtypeStruct((t, MIX_A), F32)],
        scratch_shapes=[pltpu.VMEM((nseq, 2 * WINDOW, LANES), F32), pltpu.VMEM((nseq, 2 * WINDOW, LANES), F32),
                        pltpu.VMEM((N_HEADS * seq_len, 2 * WINDOW), F32)],
        compiler_params=_cparams(1),
        name="mix_sample",
    )(x, ck, cv, gmix, win, lng, lnb, wm, bs, bkt, relb, sink, ga, gb, woa, wob)


def _mem_kv(mem, g, wk, wv):
    b, m, d = mem.shape
    return pl.pallas_call(
        _mem_kv_kernel,
        grid=(b,),
        in_specs=[pl.BlockSpec((1, m, d), lambda i: (i, 0, 0)), _full(g.shape), _full(wk.shape), _full(wv.shape)],
        out_specs=[pl.BlockSpec((1, m, MEM_INNER), lambda i: (i, 0, 0))] * 2,
        out_shape=[jax.ShapeDtypeStruct((b, m, MEM_INNER), F32)] * 2,
        compiler_params=_cparams(1),
        name="mem_kv",
    )(mem, g, wk, wv)


def _mem_outs(t, rows):
    row_spec = lambda w: pl.BlockSpec((rows, w), lambda i: (i, 0))
    specs = [row_spec(D_MODEL), row_spec(D_MODEL),
             pl.BlockSpec((2 * PEER_HEADS, rows, PEER_HALF), lambda i: (0, i, 0))]
    shapes = [jax.ShapeDtypeStruct((t, D_MODEL), F32), jax.ShapeDtypeStruct((t, D_MODEL), BF16),
              jax.ShapeDtypeStruct((2 * PEER_HEADS, t, PEER_HALF), BF16)]
    return specs, shapes


def _mem_prompt(h1, mk, mv, mwts, *, rows, seq):
    t = h1.shape[0]
    tiles_per_seq = seq // rows
    out_specs, out_shapes = _mem_outs(t, rows)
    kv_spec = pl.BlockSpec((1, N_MEM, MEM_INNER), lambda i: (i // tiles_per_seq, 0, 0))
    return pl.pallas_call(
        _mem_prompt_kernel,
        grid=(t // rows,),
        in_specs=[pl.BlockSpec((rows, D_MODEL), lambda i: (i, 0)), kv_spec, kv_spec] + [_full(w.shape) for w in mwts],
        out_specs=out_specs,
        out_shape=out_shapes,
        compiler_params=_cparams(1),
        name="mem_prompt",
    )(h1, mk, mv, *mwts)


def _mem_sample(h1, ck, cv, mwts, *, nseq, seq_len):
    t = h1.shape[0]
    rows = nseq * seq_len
    out_specs, out_shapes = _mem_outs(t, rows)
    kv_spec = pl.BlockSpec((nseq, N_MEM * MEM_HEADS, MEM_HEAD_DIM), lambda i: (i, 0, 0))
    kern = functools.partial(_mem_sample_kernel, nseq=nseq, seq_len=seq_len)
    return pl.pallas_call(
        kern,
        grid=(t // rows,),
        in_specs=[pl.BlockSpec((rows, D_MODEL), lambda i: (i, 0)), kv_spec, kv_spec] + [_full(w.shape) for w in mwts],
        out_specs=out_specs,
        out_shape=out_shapes,
        compiler_params=_cparams(1),
        name="mem_sample",
    )(h1, ck, cv, *mwts)


def _peer_prep(qp, keys, *, tokens):
    t = qp.shape[1]
    out_spec = pl.BlockSpec((PEER_HEADS, PEER_KEYS, tokens), lambda i: (0, 0, i))
    kern = functools.partial(_peer_prep_kernel, lane_tiles=tokens // LANES)
    return pl.pallas_call(
        kern,
        grid=(t // tokens,),
        in_specs=[pl.BlockSpec((2 * PEER_HEADS, tokens, PEER_HALF), lambda i: (0, i, 0)), _full(keys.shape)],
        out_specs=[out_spec] * 4,
        out_shape=[jax.ShapeDtypeStruct((PEER_HEADS, PEER_KEYS, t), dt) for dt in (BF16, BF16, F32, F32)],
        compiler_params=_cparams(1),
        name="peer_prep",
    )(qp, keys)


def _peer_ffn(xn, h2, eu, evt, prep, gfin, *, tokens, i1_per_step):
    t = xn.shape[0]
    experts = i1_per_step * PEER_KEYS
    tok_spec = pl.BlockSpec((tokens, D_MODEL), lambda i, e: (i, 0))
    prep_spec = pl.BlockSpec((PEER_HEADS, PEER_KEYS, tokens), lambda i, e: (0, 0, i))
    kern = functools.partial(_peer_ffn_kernel, tokens=tokens, i1_per_step=i1_per_step)
    return pl.pallas_call(
        kern,
        grid=(t // tokens, PEER_N // experts),
        in_specs=[tok_spec, tok_spec,
                  pl.BlockSpec((experts, D_MODEL), lambda i, e: (e, 0)),
                  pl.BlockSpec((D_MODEL, experts), lambda i, e: (0, e)),
                  prep_spec, prep_spec, prep_spec, prep_spec,
                  pl.BlockSpec(gfin.shape, lambda i, e: (0, 0))],
        out_specs=tok_spec,
        out_shape=jax.ShapeDtypeStruct((t, D_MODEL), F32),
        scratch_shapes=[pltpu.VMEM((D_MODEL, tokens), F32), pltpu.VMEM((experts, tokens), BF16)],
        compiler_params=_cparams(2),
        name="peer_ffn",
    )(xn, h2, eu, evt, *prep, gfin)


def kernel(x_prompt, x_sample, mem_prompt, cache_swa_k, cache_swa_v, cache_mem_k, cache_mem_v, norm_mix_g, w_in, ln_v_g, ln_v_b, spatial_w, spatial_b, attn_sinks, rel_bias, norm_a_out_g, norm_b_out_g, w_out, norm_mem_g, norm_memsrc_g, w_mq, w_mk, w_mv, w_mo, norm_ffn_g, peer_wq, peer_keys, peer_u, peer_v, norm_final_g):
    batch, seq, _ = x_prompt.shape
    nsamp, dec_len, _ = x_sample.shape
    l = 0

    wts = _mix_weights(norm_mix_g[l], w_in[l], ln_v_g[l], ln_v_b[l],
                       norm_a_out_g[l], norm_b_out_g[l], w_out[l])
    tril = jnp.tril(jnp.ones((CHUNK, CHUNK), F32))
    wm = spatial_w[l] * tril
    wm_p = wm.astype(BF16)
    bs_p = spatial_b[l][:, :, None]
    seqs_per_chunk = CHUNK // dec_len
    eye = jnp.eye(seqs_per_chunk, dtype=F32)
    wm_s = jnp.einsum('ab,gts->gatbs', eye, wm[:, :dec_len, :dec_len]).reshape(A_GROUPS, CHUNK, CHUNK).astype(BF16)
    bs_s = jnp.tile(spatial_b[l][:, :dec_len], (1, seqs_per_chunk))[:, :, None]
    bkt_p = _bucket_table(WINDOW)
    bkt_s = _bucket_table(dec_len)
    sink_p = jnp.repeat(attn_sinks[l], WINDOW)[:, None]
    sink_s = jnp.repeat(attn_sinks[l], dec_len)[:, None]
    mwts = (norm_mem_g[l][None], w_mq[l].astype(BF16), w_mo[l].astype(BF16), norm_ffn_g[l][None],
            peer_wq[l].astype(BF16))
    keys = peer_keys[l].astype(BF16)
    eu = peer_u[l].astype(BF16)
    evt = peer_v[l].T.astype(BF16)
    gfin = norm_final_g[None]

    xp = x_prompt.reshape(batch * seq, D_MODEL)
    h1p, kp, vp = _mix_prompt(xp, wts, wm_p, bs_p, bkt_p, rel_bias, sink_p, rows=512, seq=seq)
    mk, mv = _mem_kv(mem_prompt, norm_memsrc_g[l][None], w_mk[l].astype(BF16), w_mv[l].astype(BF16))
    h2p, xnp_, qpp = _mem_prompt(h1p, mk, mv, mwts, rows=512, seq=seq)
    prep_p = _peer_prep(qpp, keys, tokens=256)
    yp = _peer_ffn(xnp_, h2p, eu, evt, prep_p, gfin, tokens=512, i1_per_step=8)

    xs = x_sample.reshape(nsamp * dec_len, D_MODEL)
    ck = cache_swa_k.reshape(nsamp, WINDOW, LANES)
    cv = cache_swa_v.reshape(nsamp, WINDOW, LANES)
    h1s, ks, vs, chunk_v = _mix_sample(xs, ck, cv, wts, wm_s, bs_s, bkt_s, rel_bias, sink_s,
                                       nseq=seqs_per_chunk, seq_len=dec_len)
    cmk = cache_mem_k.reshape(nsamp, N_MEM * MEM_HEADS, MEM_HEAD_DIM)
    cmv = cache_mem_v.reshape(nsamp, N_MEM * MEM_HEADS, MEM_HEAD_DIM)
    h2s, xns, qps = _mem_sample(h1s, cmk, cmv, mwts, nseq=8, seq_len=dec_len)
    prep_s = _peer_prep(qps, keys, tokens=256)
    ys = _peer_ffn(xns, h2s, eu, evt, prep_s, gfin, tokens=512, i1_per_step=8)

    kp4 = kp.reshape(batch, seq, LANES)[:, -WINDOW:].reshape(batch, WINDOW, N_KV, HEAD_DIM)
    vp4 = vp.reshape(batch, seq, LANES)[:, -WINDOW:].reshape(batch, WINDOW, N_KV, HEAD_DIM)
    return (yp.reshape(batch, seq, D_MODEL),
            ys.reshape(nsamp, dec_len, D_MODEL),
            kp4[None], vp4[None],
            mk.reshape(batch, N_MEM, MEM_HEADS, MEM_HEAD_DIM)[None],
            mv.reshape(batch, N_MEM, MEM_HEADS, MEM_HEAD_DIM)[None],
            ks.reshape(nsamp, WINDOW, N_KV, HEAD_DIM)[None],
            vs.reshape(nsamp, WINDOW, N_KV, HEAD_DIM)[None],
            chunk_v.reshape(nsamp, dec_len, A_GROUPS, A_GW)[None])
```

```python
import functools
import math

import jax
import jax.numpy as jnp
import numpy as np
from jax import lax
from jax.experimental import pallas as pl
from jax.experimental.pallas import tpu as pltpu

F32 = jnp.float32
BF16 = jnp.bfloat16

D_MODEL = 1024
MIX_A = 512
A_GROUPS = 4
A_GW = 128
CHUNK = 128
N_HEADS = 8
N_KV = 2
HEAD_DIM = 64
GQA = 4
MIX_B = 512
WINDOW = 128
IN_COLS = 1792
REL_BUCKETS = 32
REL_MAX_EXACT = 16
REL_MAX_DIST = 128
N_MEM = 256
MEM_HEADS = 4
MEM_HEAD_DIM = 128
MEM_INNER = 512
PEER_HEADS = 8
PEER_KEYS = 128
PEER_N = PEER_KEYS * PEER_KEYS
PEER_HALF = 128
PEER_TOPK = 16
EPS = 1e-6
NEG = -1e30

LANES = 128
SUBLANES = 8
VMEM_LIMIT = 56 * 1024 * 1024

NT_DIMS = (((1,), (1,)), ((), ()))


def _cparams(n_axes):
    return pltpu.CompilerParams(dimension_semantics=("arbitrary",) * n_axes, vmem_limit_bytes=VMEM_LIMIT)


def _rms(x, g):
    return x * lax.rsqrt(jnp.mean(x * x, axis=-1, keepdims=True) + EPS) * g


def _gelu(x):
    return 0.5 * x * (1.0 + jnp.tanh(math.sqrt(2.0 / math.pi) * (x + 0.044715 * (x * x * x))))


def _dot(a, b):
    return jnp.dot(a, b, preferred_element_type=F32)


def _dot_nt(a, b):
    return lax.dot_general(a, b, NT_DIMS, preferred_element_type=F32)


def _mix_in(x, gmix, win, lng, lnb):
    xn = _rms(x, gmix).astype(BF16)
    h = _dot(xn, win)
    z = _gelu(h[:, :2 * MIX_A])
    u = z[:, :MIX_A]
    vs = []
    for g in range(A_GROUPS):
        seg = z[:, MIX_A + g * A_GW:MIX_A + (g + 1) * A_GW]
        mu = jnp.mean(seg, axis=-1, keepdims=True)
        cen = seg - mu
        var = jnp.mean(cen * cen, axis=-1, keepdims=True)
        vs.append(cen * lax.rsqrt(var + EPS) * lng[:, g * A_GW:(g + 1) * A_GW] + lnb[:, g * A_GW:(g + 1) * A_GW])
    v = jnp.concatenate(vs, axis=1)
    r0 = 2 * MIX_A
    q = h[:, r0:r0 + MIX_B]
    k = h[:, r0 + MIX_B:r0 + MIX_B + N_KV * HEAD_DIM]
    vb = h[:, r0 + MIX_B + N_KV * HEAD_DIM:]
    return u, v, q, k, vb


def _swa_softmax(s, mask, sk):
    s = jnp.where(mask, s, NEG)
    m = jnp.maximum(jnp.max(s, axis=-1, keepdims=True), sk)
    p = jnp.exp(s - m)
    return p / (jnp.sum(p, axis=-1, keepdims=True) + jnp.exp(sk - m))


def _fill_rel_bias(bias_scr, bkt_ref, relb_ref):
    bkt = bkt_ref[...]
    n_q = bkt.shape[0]
    for h in range(N_HEADS):
        acc = jnp.zeros(bkt.shape, F32)
        for b in range(REL_BUCKETS):
            acc = jnp.where(bkt == b, relb_ref[b:b + 1, h:h + 1], acc)
        bias_scr[h * n_q:(h + 1) * n_q, :] = acc


def _mix_prompt_kernel(x_ref, gmix_ref, win_ref, lng_ref, lnb_ref, wm_ref, bs_ref, bkt_ref, relb_ref, sink_ref,
                       ga_ref, gb_ref, woa_ref, wob_ref, h1_ref, k_ref, v_ref,
                       kprev, vprev, ya_scr, ob_scr, bias_ref, *, rows, tiles_per_seq):
    i = pl.program_id(0)

    @pl.when(i == 0)
    def _():
        _fill_rel_bias(bias_ref, bkt_ref, relb_ref)

    x = x_ref[...]
    u, v, q, k, vb = _mix_in(x, gmix_ref[...], win_ref[...], lng_ref[...], lnb_ref[...])
    k_ref[...] = k
    v_ref[...] = vb
    first_tile = (i % tiles_per_seq) == 0

    @pl.when(first_tile)
    def _():
        kprev[...] = jnp.zeros_like(kprev)
        vprev[...] = jnp.zeros_like(vprev)

    nq = N_HEADS * WINDOW
    t_idx = lax.broadcasted_iota(jnp.int32, (nq, 2 * WINDOW), 0) % WINDOW
    kj = lax.broadcasted_iota(jnp.int32, (nq, 2 * WINDOW), 1)
    band = (kj > t_idx) & (kj <= t_idx + WINDOW)
    kj_min = jnp.where(first_tile, WINDOW, 0)
    band_first = band & (kj >= kj_min)
    lane = lax.broadcasted_iota(jnp.int32, (1, LANES), 1)
    lane_c = [lane < HEAD_DIM, lane >= HEAD_DIM]
    sk = sink_ref[...]
    bias = bias_ref[...]

    for j in range(rows // WINDOW):
        r = slice(j * WINDOW, (j + 1) * WINDOW)
        for g in range(A_GROUPS):
            c = slice(g * A_GW, (g + 1) * A_GW)
            s = _dot(wm_ref[g], v[r, c].astype(BF16)) + bs_ref[g]
            ya_scr[r, c] = u[r, c] * s
        k_prev = kprev[...] if j == 0 else k[(j - 1) * WINDOW:j * WINDOW]
        v_prev = vprev[...] if j == 0 else vb[(j - 1) * WINDOW:j * WINDOW]
        kcat = jnp.concatenate([k_prev, k[r]], axis=0).astype(BF16)
        vcat = jnp.concatenate([v_prev, vb[r]], axis=0)
        qg = jnp.concatenate([q[r, g * LANES:(g + 1) * LANES] for g in range(GQA)], axis=0)
        qs = jnp.concatenate([jnp.where(lane_c[c], qg, 0.0) for c in range(N_KV)], axis=0).astype(BF16)
        s = _dot_nt(qs, kcat) * (1.0 / math.sqrt(HEAD_DIM)) + bias
        p = _swa_softmax(s, band_first if j == 0 else band, sk).astype(BF16)
        half = GQA * WINDOW
        o = (_dot(p[:half], jnp.where(lane_c[0], vcat, 0.0).astype(BF16))
             + _dot(p[half:], jnp.where(lane_c[1], vcat, 0.0).astype(BF16)))
        for g in range(GQA):
            ob_scr[r, g * LANES:(g + 1) * LANES] = o[g * WINDOW:(g + 1) * WINDOW]

    kprev[...] = k[rows - WINDOW:]
    vprev[...] = vb[rows - WINDOW:]
    ya = _rms(ya_scr[...], ga_ref[...]).astype(BF16)
    ob = _rms(ob_scr[...], gb_ref[...]).astype(BF16)
    h1_ref[...] = x + _dot(ya, woa_ref[...]) + _dot(ob, wob_ref[...])


def _mix_sample_kernel(x_ref, ck_ref, cv_ref, gmix_ref, win_ref, lng_ref, lnb_ref, wm_ref, bs_ref, bkt_ref,
                       relb_ref, sink_ref, ga_ref, gb_ref, woa_ref, wob_ref,
                       h1_ref, ko_ref, vo_ref, cvout_ref, kk_scr, vv_scr, bias_ref, *, nseq, seq_len):
    i = pl.program_id(0)
    wbuf = WINDOW

    @pl.when(i == 0)
    def _():
        kk_scr[...] = jnp.zeros_like(kk_scr)
        vv_scr[...] = jnp.zeros_like(vv_scr)
        _fill_rel_bias(bias_ref, bkt_ref, relb_ref)

    x = x_ref[...]
    u, v, q, k, vb = _mix_in(x, gmix_ref[...], win_ref[...], lng_ref[...], lnb_ref[...])
    cvout_ref[...] = v
    ya_parts = []
    for g in range(A_GROUPS):
        c = slice(g * A_GW, (g + 1) * A_GW)
        s = _dot(wm_ref[g], v[:, c].astype(BF16)) + bs_ref[g]
        ya_parts.append(u[:, c] * s)
    ya = jnp.concatenate(ya_parts, axis=1)

    kk_scr[:, 0:wbuf, :] = ck_ref[...]
    vv_scr[:, 0:wbuf, :] = cv_ref[...]
    kk_scr[:, wbuf:wbuf + seq_len, :] = k.reshape(nseq, seq_len, LANES)
    vv_scr[:, wbuf:wbuf + seq_len, :] = vb.reshape(nseq, seq_len, LANES)
    ko_ref[...] = kk_scr[:, seq_len:wbuf + seq_len, :]
    vo_ref[...] = vv_scr[:, seq_len:wbuf + seq_len, :]

    nq = N_HEADS * seq_len
    l_idx = lax.broadcasted_iota(jnp.int32, (nq, 2 * WINDOW), 0) % seq_len
    kj = lax.broadcasted_iota(jnp.int32, (nq, 2 * WINDOW), 1)
    band = (kj > l_idx) & (kj <= l_idx + wbuf)
    lane = lax.broadcasted_iota(jnp.int32, (1, 1, LANES), 2)
    lane_c = [lane < HEAD_DIM, lane >= HEAD_DIM]

    q3 = q.reshape(nseq, seq_len, MIX_B)
    qg = jnp.concatenate([q3[:, :, g * LANES:(g + 1) * LANES] for g in range(GQA)], axis=1)
    qs = jnp.concatenate([jnp.where(lane_c[c], qg, 0.0) for c in range(N_KV)], axis=1).astype(BF16)
    kk = kk_scr[...].astype(BF16)
    vv = vv_scr[...]
    s = jnp.einsum('nqe,nke->nqk', qs, kk, preferred_element_type=F32) * (1.0 / math.sqrt(HEAD_DIM))
    s = s + bias_ref[...][None]
    p = _swa_softmax(s, band[None], sink_ref[...][None]).astype(BF16)
    half = GQA * seq_len
    o = (jnp.einsum('nqk,nke->nqe', p[:, :half], jnp.where(lane_c[0], vv, 0.0).astype(BF16),
                    preferred_element_type=F32)
         + jnp.einsum('nqk,nke->nqe', p[:, half:], jnp.where(lane_c[1], vv, 0.0).astype(BF16),
                      preferred_element_type=F32))
    ob3 = jnp.concatenate([o[:, g * seq_len:(g + 1) * seq_len, :] for g in range(GQA)], axis=2)
    ob = ob3.reshape(nseq * seq_len, MIX_B)
    ya_n = _rms(ya, ga_ref[...]).astype(BF16)
    ob_n = _rms(ob, gb_ref[...]).astype(BF16)
    h1_ref[...] = x + _dot(ya_n, woa_ref[...]) + _dot(ob_n, wob_ref[...])


def _mem_kv_kernel(mem_ref, g_ref, wk_ref, wv_ref, mk_ref, mv_ref):
    mn = _rms(mem_ref[0], g_ref[...]).astype(BF16)
    mk_ref[0] = _dot(mn, wk_ref[...])
    mv_ref[0] = _dot(mn, wv_ref[...])


def _mem_tail(h1, o, wmo_ref, gffn_ref, wq_ref, h2_ref, xn_ref, qp_ref):
    h2 = h1 + _dot(o.astype(BF16), wmo_ref[...])
    h2_ref[...] = h2
    xn = _rms(h2, gffn_ref[...]).astype(BF16)
    xn_ref[...] = xn
    qp = _dot(xn, wq_ref[...]).astype(BF16)
    for hp in range(2 * PEER_HEADS):
        qp_ref[hp] = qp[:, hp * PEER_HALF:(hp + 1) * PEER_HALF]


def _mem_prompt_kernel(h1_ref, mk_ref, mv_ref, gmem_ref, wmq_ref, wmo_ref, gffn_ref, wq_ref,
                       h2_ref, xn_ref, qp_ref):
    h1 = h1_ref[...]
    q = _dot(_rms(h1, gmem_ref[...]).astype(BF16), wmq_ref[...]).astype(BF16)
    mk = mk_ref[0].astype(BF16)
    mv = mv_ref[0].astype(BF16)
    outs = []
    for hh in range(MEM_HEADS):
        c = slice(hh * MEM_HEAD_DIM, (hh + 1) * MEM_HEAD_DIM)
        s = _dot_nt(q[:, c], mk[:, c]) * (1.0 / math.sqrt(MEM_HEAD_DIM))
        m = jnp.max(s, axis=-1, keepdims=True)
        p = jnp.exp(s - m)
        p = (p / jnp.sum(p, axis=-1, keepdims=True)).astype(BF16)
        outs.append(_dot(p, mv[:, c]))
    o = jnp.concatenate(outs, axis=1)
    _mem_tail(h1, o, wmo_ref, gffn_ref, wq_ref, h2_ref, xn_ref, qp_ref)


def _mem_sample_kernel(h1_ref, ck_ref, cv_ref, gmem_ref, wmq_ref, wmo_ref, gffn_ref, wq_ref,
                       h2_ref, xn_ref, qp_ref, *, nseq, seq_len):
    h1 = h1_ref[...]
    q = _dot(_rms(h1, gmem_ref[...]).astype(BF16), wmq_ref[...])
    q3 = q.reshape(nseq, seq_len, MEM_INNER)
    qs = jnp.concatenate([q3[:, :, hh * MEM_HEAD_DIM:(hh + 1) * MEM_HEAD_DIM] for hh in range(MEM_HEADS)],
                         axis=1).astype(BF16)
    ck = ck_ref[...].astype(BF16)
    cv = cv_ref[...].astype(BF16)
    s = jnp.einsum('nqd,nkd->nqk', qs, ck, preferred_element_type=F32) * (1.0 / math.sqrt(MEM_HEAD_DIM))
    nq, nk = MEM_HEADS * seq_len, N_MEM * MEM_HEADS
    row_head = lax.broadcasted_iota(jnp.int32, (nq, nk), 0) // seq_len
    col_head = lax.broadcasted_iota(jnp.int32, (nq, nk), 1) % MEM_HEADS
    s = jnp.where((row_head == col_head)[None], s, -jnp.inf)
    m = jnp.max(s, axis=-1, keepdims=True)
    p = jnp.exp(s - m)
    p = (p / jnp.sum(p, axis=-1, keepdims=True)).astype(BF16)
    of = jnp.einsum('nqk,nkd->nqd', p, cv, preferred_element_type=F32)
    o3 = jnp.concatenate([of[:, hh * seq_len:(hh + 1) * seq_len] for hh in range(MEM_HEADS)], axis=2)
    o = o3.reshape(nseq * seq_len, MEM_INNER)
    _mem_tail(h1, o, wmo_ref, gffn_ref, wq_ref, h2_ref, xn_ref, qp_ref)


def _sort_network(n):
    out, p = [], 1
    while p < n:
        k = p
        while k >= 1:
            for j in range(k % p, n - k, 2 * k):
                for i in range(min(k, n - j - k)):
                    if (i + j) // (2 * p) == (i + j + k) // (2 * p):
                        out.append((i + j, i + j + k))
            k //= 2
        p *= 2
    return out


_SORT16 = _sort_network(PEER_TOPK)


def _compare_exchange(x, i, j):
    hi, lo = jnp.maximum(x[i], x[j]), jnp.minimum(x[i], x[j])
    x[i], x[j] = hi, lo


def _top16_sorted(slabs):
    n = len(slabs)
    x = list(slabs)
    for i, j in _SORT16:
        if j < n:
            _compare_exchange(x, i, j)
    for shift in (4, 2, 1):
        y = []
        for i in range(PEER_TOPK):
            lo = x[i] if i < n else None
            k = PEER_TOPK - 1 - i
            hi = pltpu.roll(x[k], shift, axis=0) if k < n else None
            y.append(hi if lo is None else lo if hi is None else jnp.maximum(lo, hi))
        d = PEER_TOPK // 2
        while d >= 1:
            for i in range(PEER_TOPK):
                if i & d == 0:
                    _compare_exchange(y, i, i + d)
            d //= 2
        x, n = y, PEER_TOPK
    return x


def _step_count(slabs, thresholds, below):
    out = []
    for x in slabs:
        c = jnp.zeros_like(x)
        for j, t in enumerate(thresholds):
            c = jnp.where((x < t) if below else (x >= t), float(j + 1), c)
        out.append(c)
    return out


def _peer_prep_kernel(qp_ref, keys_ref, r2_ref, e2_ref, cnt_ref, e1_ref, *, lane_tiles):
    nslab = PEER_KEYS // SUBLANES

    def per_head(h, carry):
        for lt in range(lane_tiles):
            lanes = slice(lt * LANES, (lt + 1) * LANES)
            q1 = qp_ref[2 * h, lanes, :]
            q2 = qp_ref[2 * h + 1, lanes, :]
            s1 = _dot_nt(keys_ref[h, 0], q1)
            s2 = _dot_nt(keys_ref[h, 1], q2)
            s1s = [s1[r * SUBLANES:(r + 1) * SUBLANES] for r in range(nslab)]
            s2s = [s2[r * SUBLANES:(r + 1) * SUBLANES] for r in range(nslab)]
            a = _top16_sorted(s1s)
            b = _top16_sorted(s2s)
            sub = lax.broadcasted_iota(jnp.int32, a[0].shape, 0)

            def by_sublane(vals):
                out = vals[SUBLANES - 1]
                for r in range(SUBLANES - 2, -1, -1):
                    out = jnp.where(sub == r, vals[r], out)
                return out

            a_lo, a_hi = by_sublane(a[:8]), by_sublane(a[8:])
            b_lo, b_hi = by_sublane(b[:8]), by_sublane(b[8:])
            tail = lambda x: jnp.where(sub >= 3, x, -jnp.inf)
            cand = [a_lo + b[0], a_hi + b[0], a_lo + b[1], a_lo + b[2], tail(a[0] + b_lo), a[0] + b_hi,
                    tail(a[1] + b_lo), tail(a[2] + b_lo), tail(a[3] + b_lo)]
            tops = _top16_sorted(cand)
            tau = tops[-1]
            z = jnp.ones_like(tau)
            for t in tops[1:]:
                z = z + jnp.exp(t - tops[0])
            thr = []
            for j in range(PEER_TOPK):
                t = jnp.full_like(tau, jnp.inf)
                for i in range(PEER_TOPK // (j + 1)):
                    t = jnp.where(a[i] + b[j] >= tau, a[i], t)
                thr.append(t)
            cnt = _step_count(s1s, thr, below=False)
            rank2 = _step_count(s2s, b, below=True)
            half_inv_z = 0.5 / z
            r2_ref[h, :, lanes] = jnp.concatenate(rank2, axis=0).astype(BF16)
            e2_ref[h, :, lanes] = jnp.concatenate([jnp.exp(x - b[0]) * half_inv_z for x in s2s], axis=0).astype(BF16)
            cnt_ref[h, :, lanes] = jnp.concatenate(cnt, axis=0)
            e1_ref[h, :, lanes] = jnp.concatenate([jnp.exp(x - a[0]) for x in s1s], axis=0)
        return carry

    lax.fori_loop(0, PEER_HEADS, per_head, 0)


def _peer_ffn_kernel(xn_ref, h2_ref, eu_ref, evt_ref, r2_ref, e2_ref, cnt_ref, e1_ref, gfin_ref,
                     y_ref, acc_ref, p_scr, *, tokens, i1_per_step):
    e = pl.program_id(1)

    @pl.when(e == 0)
    def _():
        acc_ref[...] = jnp.zeros_like(acc_ref)

    x = _dot_nt(eu_ref[...], xn_ref[...])
    c0 = math.sqrt(2.0 / math.pi)
    act = x + x * jnp.tanh(x * ((0.044715 * c0) * (x * x) + c0))
    i1_0 = pl.multiple_of(e * i1_per_step, i1_per_step)
    cnt_slab = [cnt_ref[h, pl.ds(i1_0, i1_per_step), :] for h in range(PEER_HEADS)]
    e1_slab = [e1_ref[h, pl.ds(i1_0, i1_per_step), :] for h in range(PEER_HEADS)]
    sub, width = SUBLANES, 2 * LANES
    for lt in range(tokens // width):
        lanes = slice(lt * width, (lt + 1) * width)
        for j in range(i1_per_step):
            w = None
            for h in range(PEER_HEADS):
                cnt = jnp.broadcast_to(cnt_slab[h][j:j + 1, lanes], (sub, width)).astype(BF16)
                e1 = jnp.broadcast_to(e1_slab[h][j:j + 1, lanes], (sub, width)).astype(BF16)
                cnt = jnp.concatenate([cnt] * (PEER_KEYS // sub), axis=0)
                e1 = jnp.concatenate([e1] * (PEER_KEYS // sub), axis=0)
                e2 = e2_ref[h, :, lanes]
                term = jnp.where(r2_ref[h, :, lanes] < cnt, e2, jnp.zeros_like(e2)) * e1
                w = term if w is None else w + term
            rows = slice(j * PEER_KEYS, (j + 1) * PEER_KEYS)
            p_scr[rows, lanes] = w * act[rows, lanes].astype(BF16)
    acc_ref[...] += _dot(evt_ref[...], p_scr[...])

    @pl.when(e == pl.num_programs(1) - 1)
    def _():
        y_ref[...] = _rms(h2_ref[...] + acc_ref[...].T, gfin_ref[...])


def _full(shape):
    return pl.BlockSpec(shape, lambda *_: (0,) * len(shape))


def _t5_bucket(d):
    n = np.maximum(d, 0)
    nf = np.maximum(n, 1).astype(np.float32)
    scaled = (np.log(nf / np.float32(REL_MAX_EXACT)) / np.float32(math.log(REL_MAX_DIST / REL_MAX_EXACT))
              * np.float32(REL_BUCKETS - REL_MAX_EXACT))
    frac = np.abs(scaled - np.round(scaled))
    capped = scaled > REL_BUCKETS - REL_MAX_EXACT
    assert np.all((frac > 1e-3) | (frac < 1e-5) | capped), "bucket boundary too close to an integer distance"
    large = np.minimum(REL_MAX_EXACT + np.floor(scaled + 1e-4).astype(np.int32), REL_BUCKETS - 1)
    return np.where(n < REL_MAX_EXACT, n, large).astype(np.int32)


def _bucket_table(n_q):
    d = (np.arange(n_q)[:, None] + WINDOW) - np.arange(2 * WINDOW)[None, :]
    return jnp.asarray(_t5_bucket(d))


def _head_perm():
    g, c, d = jnp.meshgrid(jnp.arange(GQA), jnp.arange(N_KV), jnp.arange(HEAD_DIM), indexing='ij')
    return ((c * GQA + g) * HEAD_DIM + d).reshape(-1)


def _mix_weights(norm_mix_g, w_in, ln_v_g, ln_v_b, norm_a_out_g, norm_b_out_g, w_out):
    perm = _head_perm()
    q0 = 2 * MIX_A
    win = jnp.concatenate([w_in[:, :q0], w_in[:, q0:q0 + MIX_B][:, perm], w_in[:, q0 + MIX_B:]], axis=1).astype(BF16)
    woa = w_out[:MIX_A].astype(BF16)
    wob = w_out[MIX_A:][perm].astype(BF16)
    gb = norm_b_out_g[perm][None]
    return (norm_mix_g[None], win, ln_v_g[None], ln_v_b[None], norm_a_out_g[None], gb, woa, wob)


def _mix_prompt(x, wts, wm, bs, bkt, relb, sink, *, rows, seq):
    gmix, win, lng, lnb, ga, gb, woa, wob = wts
    t = x.shape[0]
    row_spec = lambda w: pl.BlockSpec((rows, w), lambda i: (i, 0))
    kern = functools.partial(_mix_prompt_kernel, rows=rows, tiles_per_seq=seq // rows)
    return pl.pallas_call(
        kern,
        grid=(t // rows,),
        in_specs=[row_spec(D_MODEL), _full(gmix.shape), _full(win.shape), _full(lng.shape), _full(lnb.shape),
                  _full(wm.shape), _full(bs.shape), _full(bkt.shape), _full(relb.shape), _full(sink.shape),
                  _full(ga.shape), _full(gb.shape), _full(woa.shape), _full(wob.shape)],
        out_specs=[row_spec(D_MODEL), row_spec(LANES), row_spec(LANES)],
        out_shape=[jax.ShapeDtypeStruct((t, D_MODEL), F32), jax.ShapeDtypeStruct((t, LANES), F32),
                   jax.ShapeDtypeStruct((t, LANES), F32)],
        scratch_shapes=[pltpu.VMEM((WINDOW, LANES), F32), pltpu.VMEM((WINDOW, LANES), F32),
                        pltpu.VMEM((rows, MIX_A), F32), pltpu.VMEM((rows, MIX_B), F32),
                        pltpu.VMEM((N_HEADS * WINDOW, 2 * WINDOW), F32)],
        compiler_params=_cparams(1),
        name="mix_prompt",
    )(x, gmix, win, lng, lnb, wm, bs, bkt, relb, sink, ga, gb, woa, wob)


def _mix_sample(x, ck, cv, wts, wm, bs, bkt, relb, sink, *, nseq, seq_len):
    gmix, win, lng, lnb, ga, gb, woa, wob = wts
    t = x.shape[0]
    rows = nseq * seq_len
    row_spec = lambda w: pl.BlockSpec((rows, w), lambda i: (i, 0))
    seq_spec = pl.BlockSpec((nseq, WINDOW, LANES), lambda i: (i, 0, 0))
    kern = functools.partial(_mix_sample_kernel, nseq=nseq, seq_len=seq_len)
    return pl.pallas_call(
        kern,
        grid=(t // rows,),
        in_specs=[row_spec(D_MODEL), seq_spec, seq_spec, _full(gmix.shape), _full(win.shape), _full(lng.shape),
                  _full(lnb.shape), _full(wm.shape), _full(bs.shape), _full(bkt.shape), _full(relb.shape),
                  _full(sink.shape),
                  _full(ga.shape), _full(gb.shape), _full(woa.shape), _full(wob.shape)],
        out_specs=[row_spec(D_MODEL), seq_spec, seq_spec, row_spec(MIX_A)],
        out_shape=[jax.ShapeDtypeStruct((t, D_MODEL), F32),
                   jax.ShapeDtypeStruct(ck.shape, F32), jax.ShapeDtypeStruct(cv.shape, F32),
                   jax.ShapeDtypeStruct((t, MIX_A), F32)],
        scratch_shapes=[pltpu.VMEM((nseq, 2 * WINDOW, LANES), F32), pltpu.VMEM((nseq, 2 * WINDOW, LANES), F32),
                        pltpu.VMEM((N_HEADS * seq_len, 2 * WINDOW), F32)],
        compiler_params=_cparams(1),
        name="mix_sample",
    )(x, ck, cv, gmix, win, lng, lnb, wm, bs, bkt, relb, sink, ga, gb, woa, wob)


def _mem_kv(mem, g, wk, wv):
    b, m, d = mem.shape
    return pl.pallas_call(
        _mem_kv_kernel,
        grid=(b,),
        in_specs=[pl.BlockSpec((1, m, d), lambda i: (i, 0, 0)), _full(g.shape), _full(wk.shape), _full(wv.shape)],
        out_specs=[pl.BlockSpec((1, m, MEM_INNER), lambda i: (i, 0, 0))] * 2,
        out_shape=[jax.ShapeDtypeStruct((b, m, MEM_INNER), F32)] * 2,
        compiler_params=_cparams(1),
        name="mem_kv",
    )(mem, g, wk, wv)


def _mem_outs(t, rows):
    row_spec = lambda w: pl.BlockSpec((rows, w), lambda i: (i, 0))
    specs = [row_spec(D_MODEL), row_spec(D_MODEL),
             pl.BlockSpec((2 * PEER_HEADS, rows, PEER_HALF), lambda i: (0, i, 0))]
    shapes = [jax.ShapeDtypeStruct((t, D_MODEL), F32), jax.ShapeDtypeStruct((t, D_MODEL), BF16),
              jax.ShapeDtypeStruct((2 * PEER_HEADS, t, PEER_HALF), BF16)]
    return specs, shapes


def _mem_prompt(h1, mk, mv, mwts, *, rows, seq):
    t = h1.shape[0]
    tiles_per_seq = seq // rows
    out_specs, out_shapes = _mem_outs(t, rows)
    kv_spec = pl.BlockSpec((1, N_MEM, MEM_INNER), lambda i: (i // tiles_per_seq, 0, 0))
    return pl.pallas_call(
        _mem_prompt_kernel,
        grid=(t // rows,),
        in_specs=[pl.BlockSpec((rows, D_MODEL), lambda i: (i, 0)), kv_spec, kv_spec] + [_full(w.shape) for w in mwts],
        out_specs=out_specs,
        out_shape=out_shapes,
        compiler_params=_cparams(1),
        name="mem_prompt",
    )(h1, mk, mv, *mwts)


def _mem_sample(h1, ck, cv, mwts, *, nseq, seq_len):
    t = h1.shape[0]
    rows = nseq * seq_len
    out_specs, out_shapes = _mem_outs(t, rows)
    kv_spec = pl.BlockSpec((nseq, N_MEM * MEM_HEADS, MEM_HEAD_DIM), lambda i: (i, 0, 0))
    kern = functools.partial(_mem_sample_kernel, nseq=nseq, seq_len=seq_len)
    return pl.pallas_call(
        kern,
        grid=(t // rows,),
        in_specs=[pl.BlockSpec((rows, D_MODEL), lambda i: (i, 0)), kv_spec, kv_spec] + [_full(w.shape) for w in mwts],
        out_specs=out_specs,
        out_shape=out_shapes,
        compiler_params=_cparams(1),
        name="mem_sample",
    )(h1, ck, cv, *mwts)


def _peer_prep(qp, keys, *, tokens):
    t = qp.shape[1]
    out_spec = pl.BlockSpec((PEER_HEADS, PEER_KEYS, tokens), lambda i: (0, 0, i))
    kern = functools.partial(_peer_prep_kernel, lane_tiles=tokens // LANES)
    return pl.pallas_call(
        kern,
        grid=(t // tokens,),
        in_specs=[pl.BlockSpec((2 * PEER_HEADS, tokens, PEER_HALF), lambda i: (0, i, 0)), _full(keys.shape)],
        out_specs=[out_spec] * 4,
        out_shape=[jax.ShapeDtypeStruct((PEER_HEADS, PEER_KEYS, t), dt) for dt in (BF16, BF16, F32, F32)],
        compiler_params=_cparams(1),
        name="peer_prep",
    )(qp, keys)


def _peer_ffn(xn, h2, eu, evt, prep, gfin, *, tokens, i1_per_step):
    t = xn.shape[0]
    experts = i1_per_step * PEER_KEYS
    tok_spec = pl.BlockSpec((tokens, D_MODEL), lambda i, e: (i, 0))
    prep_spec = pl.BlockSpec((PEER_HEADS, PEER_KEYS, tokens), lambda i, e: (0, 0, i))
    kern = functools.partial(_peer_ffn_kernel, tokens=tokens, i1_per_step=i1_per_step)
    return pl.pallas_call(
        kern,
        grid=(t // tokens, PEER_N // experts),
        in_specs=[tok_spec, tok_spec,
                  pl.BlockSpec((experts, D_MODEL), lambda i, e: (e, 0)),
                  pl.BlockSpec((D_MODEL, experts), lambda i, e: (0, e)),
                  prep_spec, prep_spec, prep_spec, prep_spec,
                  pl.BlockSpec(gfin.shape, lambda i, e: (0, 0))],
        out_specs=tok_spec,
        out_shape=jax.ShapeDtypeStruct((t, D_MODEL), F32),
        scratch_shapes=[pltpu.VMEM((D_MODEL, tokens), F32), pltpu.VMEM((experts, tokens), BF16)],
        compiler_params=_cparams(2),
        name="peer_ffn",
    )(xn, h2, eu, evt, *prep, gfin)


def kernel(x_prompt, x_sample, mem_prompt, cache_swa_k, cache_swa_v, cache_mem_k, cache_mem_v, norm_mix_g, w_in, ln_v_g, ln_v_b, spatial_w, spatial_b, attn_sinks, rel_bias, norm_a_out_g, norm_b_out_g, w_out, norm_mem_g, norm_memsrc_g, w_mq, w_mk, w_mv, w_mo, norm_ffn_g, peer_wq, peer_keys, peer_u, peer_v, norm_final_g):
    batch, seq, _ = x_prompt.shape
    nsamp, dec_len, _ = x_sample.shape
    l = 0

    wts = _mix_weights(norm_mix_g[l], w_in[l], ln_v_g[l], ln_v_b[l],
                       norm_a_out_g[l], norm_b_out_g[l], w_out[l])
    tril = jnp.tril(jnp.ones((CHUNK, CHUNK), F32))
    wm = spatial_w[l] * tril
    wm_p = wm.astype(BF16)
    bs_p = spatial_b[l][:, :, None]
    seqs_per_chunk = CHUNK // dec_len
    eye = jnp.eye(seqs_per_chunk, dtype=F32)
    wm_s = jnp.einsum('ab,gts->gatbs', eye, wm[:, :dec_len, :dec_len]).reshape(A_GROUPS, CHUNK, CHUNK).astype(BF16)
    bs_s = jnp.tile(spatial_b[l][:, :dec_len], (1, seqs_per_chunk))[:, :, None]
    bkt_p = _bucket_table(WINDOW)
    bkt_s = _bucket_table(dec_len)
    sink_p = jnp.repeat(attn_sinks[l], WINDOW)[:, None]
    sink_s = jnp.repeat(attn_sinks[l], dec_len)[:, None]
    mwts = (norm_mem_g[l][None], w_mq[l].astype(BF16), w_mo[l].astype(BF16), norm_ffn_g[l][None],
            peer_wq[l].astype(BF16))
    keys = peer_keys[l].astype(BF16)
    eu = peer_u[l].astype(BF16)
    evt = peer_v[l].T.astype(BF16)
    gfin = norm_final_g[None]

    xp = x_prompt.reshape(batch * seq, D_MODEL)
    h1p, kp, vp = _mix_prompt(xp, wts, wm_p, bs_p, bkt_p, rel_bias, sink_p, rows=512, seq=seq)
    mk, mv = _mem_kv(mem_prompt, norm_memsrc_g[l][None], w_mk[l].astype(BF16), w_mv[l].astype(BF16))
    h2p, xnp_, qpp = _mem_prompt(h1p, mk, mv, mwts, rows=512, seq=seq)
    prep_p = _peer_prep(qpp, keys, tokens=512)
    yp = _peer_ffn(xnp_, h2p, eu, evt, prep_p, gfin, tokens=512, i1_per_step=16)

    xs = x_sample.reshape(nsamp * dec_len, D_MODEL)
    ck = cache_swa_k.reshape(nsamp, WINDOW, LANES)
    cv = cache_swa_v.reshape(nsamp, WINDOW, LANES)
    h1s, ks, vs, chunk_v = _mix_sample(xs, ck, cv, wts, wm_s, bs_s, bkt_s, rel_bias, sink_s,
                                       nseq=seqs_per_chunk, seq_len=dec_len)
    cmk = cache_mem_k.reshape(nsamp, N_MEM * MEM_HEADS, MEM_HEAD_DIM)
    cmv = cache_mem_v.reshape(nsamp, N_MEM * MEM_HEADS, MEM_HEAD_DIM)
    h2s, xns, qps = _mem_sample(h1s, cmk, cmv, mwts, nseq=8, seq_len=dec_len)
    prep_s = _peer_prep(qps, keys, tokens=512)
    ys = _peer_ffn(xns, h2s, eu, evt, prep_s, gfin, tokens=512, i1_per_step=16)

    kp4 = kp.reshape(batch, seq, LANES)[:, -WINDOW:].reshape(batch, WINDOW, N_KV, HEAD_DIM)
    vp4 = vp.reshape(batch, seq, LANES)[:, -WINDOW:].reshape(batch, WINDOW, N_KV, HEAD_DIM)
    return (yp.reshape(batch, seq, D_MODEL),
            ys.reshape(nsamp, dec_len, D_MODEL),
            kp4[None], vp4[None],
            mk.reshape(batch, N_MEM, MEM_HEADS, MEM_HEAD_DIM)[None],
            mv.reshape(batch, N_MEM, MEM_HEADS, MEM_HEAD_DIM)[None],
            ks.reshape(nsamp, WINDOW, N_KV, HEAD_DIM)[None],
            vs.reshape(nsamp, WINDOW, N_KV, HEAD_DIM)[None],
            chunk_v.reshape(nsamp, dec_len, A_GROUPS, A_GW)[None])
```

```python
import functools
import math

import jax
import jax.numpy as jnp
import numpy as np
from jax import lax
from jax.experimental import pallas as pl
from jax.experimental.pallas import tpu as pltpu

F32 = jnp.float32
BF16 = jnp.bfloat16

D_MODEL = 1024
MIX_A = 512
A_GROUPS = 4
A_GW = 128
CHUNK = 128
N_HEADS = 8
N_KV = 2
HEAD_DIM = 64
GQA = 4
MIX_B = 512
WINDOW = 128
IN_COLS = 1792
REL_BUCKETS = 32
REL_MAX_EXACT = 16
REL_MAX_DIST = 128
N_MEM = 256
MEM_HEADS = 4
MEM_HEAD_DIM = 128
MEM_INNER = 512
PEER_HEADS = 8
PEER_KEYS = 128
PEER_N = PEER_KEYS * PEER_KEYS
PEER_HALF = 128
PEER_TOPK = 16
EPS = 1e-6
NEG = -1e30

LANES = 128
SUBLANES = 8
VMEM_LIMIT = 56 * 1024 * 1024

NT_DIMS = (((1,), (1,)), ((), ()))


def _cparams(n_axes):
    return pltpu.CompilerParams(dimension_semantics=("arbitrary",) * n_axes, vmem_limit_bytes=VMEM_LIMIT)


def _rms(x, g):
    return x * lax.rsqrt(jnp.mean(x * x, axis=-1, keepdims=True) + EPS) * g


def _gelu(x):
    return 0.5 * x * (1.0 + jnp.tanh(math.sqrt(2.0 / math.pi) * (x + 0.044715 * (x * x * x))))


def _dot(a, b):
    return jnp.dot(a, b, preferred_element_type=F32)


def _dot_nt(a, b):
    return lax.dot_general(a, b, NT_DIMS, preferred_element_type=F32)


def _mix_in(x, gmix, win, lng, lnb):
    xn = _rms(x, gmix).astype(BF16)
    h = _dot(xn, win)
    z = _gelu(h[:, :2 * MIX_A])
    u = z[:, :MIX_A]
    vs = []
    for g in range(A_GROUPS):
        seg = z[:, MIX_A + g * A_GW:MIX_A + (g + 1) * A_GW]
        mu = jnp.mean(seg, axis=-1, keepdims=True)
        cen = seg - mu
        var = jnp.mean(cen * cen, axis=-1, keepdims=True)
        vs.append(cen * lax.rsqrt(var + EPS) * lng[:, g * A_GW:(g + 1) * A_GW] + lnb[:, g * A_GW:(g + 1) * A_GW])
    v = jnp.concatenate(vs, axis=1)
    r0 = 2 * MIX_A
    q = h[:, r0:r0 + MIX_B]
    k = h[:, r0 + MIX_B:r0 + MIX_B + N_KV * HEAD_DIM]
    vb = h[:, r0 + MIX_B + N_KV * HEAD_DIM:]
    return u, v, q, k, vb


def _swa_softmax(s, mask, sk):
    s = jnp.where(mask, s, NEG)
    m = jnp.maximum(jnp.max(s, axis=-1, keepdims=True), sk)
    p = jnp.exp(s - m)
    return p / (jnp.sum(p, axis=-1, keepdims=True) + jnp.exp(sk - m))


def _fill_rel_bias(bias_scr, bkt_ref, relb_ref):
    bkt = bkt_ref[...]
    n_q = bkt.shape[0]
    for h in range(N_HEADS):
        acc = jnp.zeros(bkt.shape, F32)
        for b in range(REL_BUCKETS):
            acc = jnp.where(bkt == b, relb_ref[b:b + 1, h:h + 1], acc)
        bias_scr[h * n_q:(h + 1) * n_q, :] = acc


def _mix_prompt_kernel(x_ref, gmix_ref, win_ref, lng_ref, lnb_ref, wm_ref, bs_ref, bkt_ref, relb_ref, sink_ref,
                       ga_ref, gb_ref, woa_ref, wob_ref, h1_ref, k_ref, v_ref,
                       kprev, vprev, ya_scr, ob_scr, bias_ref, *, rows, tiles_per_seq):
    i = pl.program_id(0)

    @pl.when(i == 0)
    def _():
        _fill_rel_bias(bias_ref, bkt_ref, relb_ref)

    x = x_ref[...]
    u, v, q, k, vb = _mix_in(x, gmix_ref[...], win_ref[...], lng_ref[...], lnb_ref[...])
    k_ref[...] = k
    v_ref[...] = vb
    first_tile = (i % tiles_per_seq) == 0

    @pl.when(first_tile)
    def _():
        kprev[...] = jnp.zeros_like(kprev)
        vprev[...] = jnp.zeros_like(vprev)

    nq = N_HEADS * WINDOW
    t_idx = lax.broadcasted_iota(jnp.int32, (nq, 2 * WINDOW), 0) % WINDOW
    kj = lax.broadcasted_iota(jnp.int32, (nq, 2 * WINDOW), 1)
    band = (kj > t_idx) & (kj <= t_idx + WINDOW)
    kj_min = jnp.where(first_tile, WINDOW, 0)
    band_first = band & (kj >= kj_min)
    lane = lax.broadcasted_iota(jnp.int32, (1, LANES), 1)
    lane_c = [lane < HEAD_DIM, lane >= HEAD_DIM]
    sk = sink_ref[...]
    bias = bias_ref[...]

    nblk = rows // WINDOW

    def block_rows(j):
        return slice(j * WINDOW, (j + 1) * WINDOW)

    def scores(j):
        r = block_rows(j)
        k_prev = kprev[...] if j == 0 else k[block_rows(j - 1)]
        kcat = jnp.concatenate([k_prev, k[r]], axis=0).astype(BF16)
        qg = jnp.concatenate([q[r, g * LANES:(g + 1) * LANES] for g in range(GQA)], axis=0)
        qs = jnp.concatenate([jnp.where(lane_c[c], qg, 0.0) for c in range(N_KV)], axis=0).astype(BF16)
        return _dot_nt(qs, kcat) * (1.0 / math.sqrt(HEAD_DIM)) + bias

    s_next = scores(0)
    for j in range(nblk):
        r = block_rows(j)
        s = s_next
        if j + 1 < nblk:
            s_next = scores(j + 1)
        for g in range(A_GROUPS):
            c = slice(g * A_GW, (g + 1) * A_GW)
            gate = _dot(wm_ref[g], v[r, c].astype(BF16)) + bs_ref[g]
            ya_scr[r, c] = u[r, c] * gate
        v_prev = vprev[...] if j == 0 else vb[block_rows(j - 1)]
        vcat = jnp.concatenate([v_prev, vb[r]], axis=0)
        p = _swa_softmax(s, band_first if j == 0 else band, sk).astype(BF16)
        half = GQA * WINDOW
        o = (_dot(p[:half], jnp.where(lane_c[0], vcat, 0.0).astype(BF16))
             + _dot(p[half:], jnp.where(lane_c[1], vcat, 0.0).astype(BF16)))
        for g in range(GQA):
            ob_scr[r, g * LANES:(g + 1) * LANES] = o[g * WINDOW:(g + 1) * WINDOW]

    kprev[...] = k[rows - WINDOW:]
    vprev[...] = vb[rows - WINDOW:]
    ya = _rms(ya_scr[...], ga_ref[...]).astype(BF16)
    ob = _rms(ob_scr[...], gb_ref[...]).astype(BF16)
    h1_ref[...] = x + _dot(ya, woa_ref[...]) + _dot(ob, wob_ref[...])


def _mix_sample_kernel(x_ref, ck_ref, cv_ref, gmix_ref, win_ref, lng_ref, lnb_ref, wm_ref, bs_ref, bkt_ref,
                       relb_ref, sink_ref, ga_ref, gb_ref, woa_ref, wob_ref,
                       h1_ref, ko_ref, vo_ref, cvout_ref, kk_scr, vv_scr, bias_ref, *, nseq, seq_len):
    i = pl.program_id(0)
    wbuf = WINDOW

    @pl.when(i == 0)
    def _():
        kk_scr[...] = jnp.zeros_like(kk_scr)
        vv_scr[...] = jnp.zeros_like(vv_scr)
        _fill_rel_bias(bias_ref, bkt_ref, relb_ref)

    x = x_ref[...]
    u, v, q, k, vb = _mix_in(x, gmix_ref[...], win_ref[...], lng_ref[...], lnb_ref[...])
    cvout_ref[...] = v
    ya_parts = []
    for g in range(A_GROUPS):
        c = slice(g * A_GW, (g + 1) * A_GW)
        s = _dot(wm_ref[g], v[:, c].astype(BF16)) + bs_ref[g]
        ya_parts.append(u[:, c] * s)
    ya = jnp.concatenate(ya_parts, axis=1)

    kk_scr[:, 0:wbuf, :] = ck_ref[...]
    vv_scr[:, 0:wbuf, :] = cv_ref[...]
    kk_scr[:, wbuf:wbuf + seq_len, :] = k.reshape(nseq, seq_len, LANES)
    vv_scr[:, wbuf:wbuf + seq_len, :] = vb.reshape(nseq, seq_len, LANES)
    ko_ref[...] = kk_scr[:, seq_len:wbuf + seq_len, :]
    vo_ref[...] = vv_scr[:, seq_len:wbuf + seq_len, :]

    nq = N_HEADS * seq_len
    l_idx = lax.broadcasted_iota(jnp.int32, (nq, 2 * WINDOW), 0) % seq_len
    kj = lax.broadcasted_iota(jnp.int32, (nq, 2 * WINDOW), 1)
    band = (kj > l_idx) & (kj <= l_idx + wbuf)
    lane = lax.broadcasted_iota(jnp.int32, (1, 1, LANES), 2)
    lane_c = [lane < HEAD_DIM, lane >= HEAD_DIM]

    q3 = q.reshape(nseq, seq_len, MIX_B)
    qg = jnp.concatenate([q3[:, :, g * LANES:(g + 1) * LANES] for g in range(GQA)], axis=1)
    qs = jnp.concatenate([jnp.where(lane_c[c], qg, 0.0) for c in range(N_KV)], axis=1).astype(BF16)
    kk = kk_scr[...].astype(BF16)
    vv = vv_scr[...]
    s = jnp.einsum('nqe,nke->nqk', qs, kk, preferred_element_type=F32) * (1.0 / math.sqrt(HEAD_DIM))
    s = s + bias_ref[...][None]
    p = _swa_softmax(s, band[None], sink_ref[...][None]).astype(BF16)
    half = GQA * seq_len
    o = (jnp.einsum('nqk,nke->nqe', p[:, :half], jnp.where(lane_c[0], vv, 0.0).astype(BF16),
                    preferred_element_type=F32)
         + jnp.einsum('nqk,nke->nqe', p[:, half:], jnp.where(lane_c[1], vv, 0.0).astype(BF16),
                      preferred_element_type=F32))
    ob3 = jnp.concatenate([o[:, g * seq_len:(g + 1) * seq_len, :] for g in range(GQA)], axis=2)
    ob = ob3.reshape(nseq * seq_len, MIX_B)
    ya_n = _rms(ya, ga_ref[...]).astype(BF16)
    ob_n = _rms(ob, gb_ref[...]).astype(BF16)
    h1_ref[...] = x + _dot(ya_n, woa_ref[...]) + _dot(ob_n, wob_ref[...])


def _mem_kv_kernel(mem_ref, g_ref, wk_ref, wv_ref, mk_ref, mv_ref):
    mn = _rms(mem_ref[0], g_ref[...]).astype(BF16)
    mk_ref[0] = _dot(mn, wk_ref[...])
    mv_ref[0] = _dot(mn, wv_ref[...])


def _mem_tail(h1, o, wmo_ref, gffn_ref, wq_ref, h2_ref, xn_ref, qp_ref):
    h2 = h1 + _dot(o.astype(BF16), wmo_ref[...])
    h2_ref[...] = h2
    xn = _rms(h2, gffn_ref[...]).astype(BF16)
    xn_ref[...] = xn
    qp = _dot(xn, wq_ref[...]).astype(BF16)
    for hp in range(2 * PEER_HEADS):
        qp_ref[hp] = qp[:, hp * PEER_HALF:(hp + 1) * PEER_HALF]


def _mem_prompt_kernel(h1_ref, mk_ref, mv_ref, gmem_ref, wmq_ref, wmo_ref, gffn_ref, wq_ref,
                       h2_ref, xn_ref, qp_ref):
    h1 = h1_ref[...]
    q = _dot(_rms(h1, gmem_ref[...]).astype(BF16), wmq_ref[...]).astype(BF16)
    mk = mk_ref[0].astype(BF16)
    mv = mv_ref[0].astype(BF16)
    outs = []
    for hh in range(MEM_HEADS):
        c = slice(hh * MEM_HEAD_DIM, (hh + 1) * MEM_HEAD_DIM)
        s = _dot_nt(q[:, c], mk[:, c]) * (1.0 / math.sqrt(MEM_HEAD_DIM))
        m = jnp.max(s, axis=-1, keepdims=True)
        p = jnp.exp(s - m)
        p = (p / jnp.sum(p, axis=-1, keepdims=True)).astype(BF16)
        outs.append(_dot(p, mv[:, c]))
    o = jnp.concatenate(outs, axis=1)
    _mem_tail(h1, o, wmo_ref, gffn_ref, wq_ref, h2_ref, xn_ref, qp_ref)


def _mem_sample_kernel(h1_ref, ck_ref, cv_ref, gmem_ref, wmq_ref, wmo_ref, gffn_ref, wq_ref,
                       h2_ref, xn_ref, qp_ref, *, nseq, seq_len):
    h1 = h1_ref[...]
    q = _dot(_rms(h1, gmem_ref[...]).astype(BF16), wmq_ref[...])
    q3 = q.reshape(nseq, seq_len, MEM_INNER)
    qs = jnp.concatenate([q3[:, :, hh * MEM_HEAD_DIM:(hh + 1) * MEM_HEAD_DIM] for hh in range(MEM_HEADS)],
                         axis=1).astype(BF16)
    ck = ck_ref[...].astype(BF16)
    cv = cv_ref[...].astype(BF16)
    s = jnp.einsum('nqd,nkd->nqk', qs, ck, preferred_element_type=F32) * (1.0 / math.sqrt(MEM_HEAD_DIM))
    nq, nk = MEM_HEADS * seq_len, N_MEM * MEM_HEADS
    row_head = lax.broadcasted_iota(jnp.int32, (nq, nk), 0) // seq_len
    col_head = lax.broadcasted_iota(jnp.int32, (nq, nk), 1) % MEM_HEADS
    s = jnp.where((row_head == col_head)[None], s, -jnp.inf)
    m = jnp.max(s, axis=-1, keepdims=True)
    p = jnp.exp(s - m)
    p = (p / jnp.sum(p, axis=-1, keepdims=True)).astype(BF16)
    of = jnp.einsum('nqk,nkd->nqd', p, cv, preferred_element_type=F32)
    o3 = jnp.concatenate([of[:, hh * seq_len:(hh + 1) * seq_len] for hh in range(MEM_HEADS)], axis=2)
    o = o3.reshape(nseq * seq_len, MEM_INNER)
    _mem_tail(h1, o, wmo_ref, gffn_ref, wq_ref, h2_ref, xn_ref, qp_ref)


def _sort_network(n):
    out, p = [], 1
    while p < n:
        k = p
        while k >= 1:
            for j in range(k % p, n - k, 2 * k):
                for i in range(min(k, n - j - k)):
                    if (i + j) // (2 * p) == (i + j + k) // (2 * p):
                        out.append((i + j, i + j + k))
            k //= 2
        p *= 2
    return out


_SORT16 = _sort_network(PEER_TOPK)


def _compare_exchange(x, i, j):
    hi, lo = jnp.maximum(x[i], x[j]), jnp.minimum(x[i], x[j])
    x[i], x[j] = hi, lo


def _top16_sorted(slabs):
    n = len(slabs)
    x = list(slabs)
    for i, j in _SORT16:
        if j < n:
            _compare_exchange(x, i, j)
    for shift in (4, 2, 1):
        y = []
        for i in range(PEER_TOPK):
            lo = x[i] if i < n else None
            k = PEER_TOPK - 1 - i
            hi = pltpu.roll(x[k], shift, axis=0) if k < n else None
            y.append(hi if lo is None else lo if hi is None else jnp.maximum(lo, hi))
        d = PEER_TOPK // 2
        while d >= 1:
            for i in range(PEER_TOPK):
                if i & d == 0:
                    _compare_exchange(y, i, i + d)
            d //= 2
        x, n = y, PEER_TOPK
    return x


def _step_count(slabs, thresholds, below):
    out = []
    for x in slabs:
        c = jnp.zeros_like(x)
        for j, t in enumerate(thresholds):
            c = jnp.where((x < t) if below else (x >= t), float(j + 1), c)
        out.append(c)
    return out


def _peer_prep_kernel(qp_ref, keys_ref, r2_ref, e2_ref, cnt_ref, e1_ref, *, lane_tiles):
    nslab = PEER_KEYS // SUBLANES

    def per_head(h, carry):
        for lt in range(lane_tiles):
            lanes = slice(lt * LANES, (lt + 1) * LANES)
            q1 = qp_ref[2 * h, lanes, :]
            q2 = qp_ref[2 * h + 1, lanes, :]
            s1 = _dot_nt(keys_ref[h, 0], q1)
            s2 = _dot_nt(keys_ref[h, 1], q2)
            s1s = [s1[r * SUBLANES:(r + 1) * SUBLANES] for r in range(nslab)]
            s2s = [s2[r * SUBLANES:(r + 1) * SUBLANES] for r in range(nslab)]
            a = _top16_sorted(s1s)
            b = _top16_sorted(s2s)
            sub = lax.broadcasted_iota(jnp.int32, a[0].shape, 0)

            def by_sublane(vals):
                out = vals[SUBLANES - 1]
                for r in range(SUBLANES - 2, -1, -1):
                    out = jnp.where(sub == r, vals[r], out)
                return out

            a_lo, a_hi = by_sublane(a[:8]), by_sublane(a[8:])
            b_lo, b_hi = by_sublane(b[:8]), by_sublane(b[8:])
            tail = lambda x: jnp.where(sub >= 3, x, -jnp.inf)
            cand = [a_lo + b[0], a_hi + b[0], a_lo + b[1], a_lo + b[2], tail(a[0] + b_lo), a[0] + b_hi,
                    tail(a[1] + b_lo), tail(a[2] + b_lo), tail(a[3] + b_lo)]
            tops = _top16_sorted(cand)
            tau = tops[-1]
            z = jnp.ones_like(tau)
            for t in tops[1:]:
                z = z + jnp.exp(t - tops[0])
            thr = []
            for j in range(PEER_TOPK):
                t = jnp.full_like(tau, jnp.inf)
                for i in range(PEER_TOPK // (j + 1)):
                    t = jnp.where(a[i] + b[j] >= tau, a[i], t)
                thr.append(t)
            cnt = _step_count(s1s, thr, below=False)
            rank2 = _step_count(s2s, b, below=True)
            half_inv_z = 0.5 / z
            r2_ref[h, :, lanes] = jnp.concatenate(rank2, axis=0).astype(BF16)
            e2_ref[h, :, lanes] = jnp.concatenate([jnp.exp(x - b[0]) * half_inv_z for x in s2s], axis=0).astype(BF16)
            cnt_ref[h, :, lanes] = jnp.concatenate(cnt, axis=0)
            e1_ref[h, :, lanes] = jnp.concatenate([jnp.exp(x - a[0]) for x in s1s], axis=0)
        return carry

    lax.fori_loop(0, PEER_HEADS, per_head, 0)


def _peer_weight_rows(j, row, lanes, cnt_slab, e1_slab, x_ref, p_ref, r2_ref, e2_ref):
    sub, width = SUBLANES, lanes.stop - lanes.start
    rows = slice(j * PEER_KEYS, (j + 1) * PEER_KEYS)
    x = x_ref[rows, lanes]
    c0 = math.sqrt(2.0 / math.pi)
    act = (x + x * jnp.tanh(x * ((0.044715 * c0) * (x * x) + c0))).astype(BF16)
    w = None
    for h in range(PEER_HEADS):
        cnt = jnp.broadcast_to(cnt_slab[h][row:row + 1, lanes], (sub, width)).astype(BF16)
        e1 = jnp.broadcast_to(e1_slab[h][row:row + 1, lanes], (sub, width)).astype(BF16)
        cnt = jnp.concatenate([cnt] * (PEER_KEYS // sub), axis=0)
        e1 = jnp.concatenate([e1] * (PEER_KEYS // sub), axis=0)
        e2 = e2_ref[h, :, lanes]
        term = jnp.where(r2_ref[h, :, lanes] < cnt, e2, jnp.zeros_like(e2)) * e1
        w = term if w is None else w + term
    p_ref[rows, lanes] = w * act


def _interleave(mxu_chunks, vpu_chunks):
    per = -(-len(vpu_chunks) // max(len(mxu_chunks), 1))
    for c, mxu in enumerate(mxu_chunks):
        mxu()
        for vpu in vpu_chunks[c * per:(c + 1) * per]:
            vpu()
    for vpu in vpu_chunks[len(mxu_chunks) * per:]:
        vpu()


def _peer_ffn_kernel(xn_ref, h2_ref, eu_ref, evt_ref, r2_ref, e2_ref, cnt_ref, e1_ref, gfin_ref,
                     y_ref, acc_ref, x_a, x_b, p_a, p_b, *, tokens, i1_per_step):
    e = pl.program_id(1)
    half_keys = i1_per_step // 2
    half = half_keys * PEER_KEYS
    width = 2 * LANES
    n_lt = tokens // width
    m_rows = half // 2
    d_rows = D_MODEL // 2

    @pl.when(e == 0)
    def _():
        acc_ref[...] = jnp.zeros_like(acc_ref)

    i1_0 = pl.multiple_of(e * i1_per_step, i1_per_step)
    cnt_slab = [cnt_ref[h, pl.ds(i1_0, i1_per_step), :] for h in range(PEER_HEADS)]
    e1_slab = [e1_ref[h, pl.ds(i1_0, i1_per_step), :] for h in range(PEER_HEADS)]

    def u_product(hf, x_w, mh, lt):
        rows = slice(mh * m_rows, (mh + 1) * m_rows)
        lanes = slice(lt * width, (lt + 1) * width)
        x_w[rows, lanes] = _dot_nt(eu_ref[hf * half + rows.start:hf * half + rows.stop, :], xn_ref[lanes, :])

    def v_product(hf, p_r, dh, lt):
        rows = slice(dh * d_rows, (dh + 1) * d_rows)
        lanes = slice(lt * width, (lt + 1) * width)
        acc_ref[rows, lanes] += _dot(evt_ref[rows, hf * half:(hf + 1) * half], p_r[:, lanes])

    def weights(hf, x_r, p_w):
        return [functools.partial(_peer_weight_rows, j, hf * half_keys + j, slice(lt * width, (lt + 1) * width),
                                  cnt_slab, e1_slab, x_r, p_w, r2_ref, e2_ref)
                for lt in range(n_lt) for j in range(half_keys)]

    def products(fn, hf, ref):
        return [functools.partial(fn, hf, ref, rh, lt) for lt in range(n_lt) for rh in range(2)]

    _interleave(products(u_product, 0, x_a), [])
    _interleave(products(u_product, 1, x_b), weights(0, x_a, p_a))
    _interleave(products(v_product, 0, p_a), weights(1, x_b, p_b))
    _interleave(products(v_product, 1, p_b), [])

    @pl.when(e == pl.num_programs(1) - 1)
    def _():
        y_ref[...] = _rms(h2_ref[...] + acc_ref[...].T, gfin_ref[...])


def _full(shape):
    return pl.BlockSpec(shape, lambda *_: (0,) * len(shape))


def _t5_bucket(d):
    n = np.maximum(d, 0)
    nf = np.maximum(n, 1).astype(np.float32)
    scaled = (np.log(nf / np.float32(REL_MAX_EXACT)) / np.float32(math.log(REL_MAX_DIST / REL_MAX_EXACT))
              * np.float32(REL_BUCKETS - REL_MAX_EXACT))
    frac = np.abs(scaled - np.round(scaled))
    capped = scaled > REL_BUCKETS - REL_MAX_EXACT
    assert np.all((frac > 1e-3) | (frac < 1e-5) | capped), "bucket boundary too close to an integer distance"
    large = np.minimum(REL_MAX_EXACT + np.floor(scaled + 1e-4).astype(np.int32), REL_BUCKETS - 1)
    return np.where(n < REL_MAX_EXACT, n, large).astype(np.int32)


def _bucket_table(n_q):
    d = (np.arange(n_q)[:, None] + WINDOW) - np.arange(2 * WINDOW)[None, :]
    return jnp.asarray(_t5_bucket(d))


def _head_perm():
    g, c, d = jnp.meshgrid(jnp.arange(GQA), jnp.arange(N_KV), jnp.arange(HEAD_DIM), indexing='ij')
    return ((c * GQA + g) * HEAD_DIM + d).reshape(-1)


def _mix_weights(norm_mix_g, w_in, ln_v_g, ln_v_b, norm_a_out_g, norm_b_out_g, w_out):
    perm = _head_perm()
    q0 = 2 * MIX_A
    win = jnp.concatenate([w_in[:, :q0], w_in[:, q0:q0 + MIX_B][:, perm], w_in[:, q0 + MIX_B:]], axis=1).astype(BF16)
    woa = w_out[:MIX_A].astype(BF16)
    wob = w_out[MIX_A:][perm].astype(BF16)
    gb = norm_b_out_g[perm][None]
    return (norm_mix_g[None], win, ln_v_g[None], ln_v_b[None], norm_a_out_g[None], gb, woa, wob)


def _mix_prompt(x, wts, wm, bs, bkt, relb, sink, *, rows, seq):
    gmix, win, lng, lnb, ga, gb, woa, wob = wts
    t = x.shape[0]
    row_spec = lambda w: pl.BlockSpec((rows, w), lambda i: (i, 0))
    kern = functools.partial(_mix_prompt_kernel, rows=rows, tiles_per_seq=seq // rows)
    return pl.pallas_call(
        kern,
        grid=(t // rows,),
        in_specs=[row_spec(D_MODEL), _full(gmix.shape), _full(win.shape), _full(lng.shape), _full(lnb.shape),
                  _full(wm.shape), _full(bs.shape), _full(bkt.shape), _full(relb.shape), _full(sink.shape),
                  _full(ga.shape), _full(gb.shape), _full(woa.shape), _full(wob.shape)],
        out_specs=[row_spec(D_MODEL), row_spec(LANES), row_spec(LANES)],
        out_shape=[jax.ShapeDtypeStruct((t, D_MODEL), F32), jax.ShapeDtypeStruct((t, LANES), F32),
                   jax.ShapeDtypeStruct((t, LANES), F32)],
        scratch_shapes=[pltpu.VMEM((WINDOW, LANES), F32), pltpu.VMEM((WINDOW, LANES), F32),
                        pltpu.VMEM((rows, MIX_A), F32), pltpu.VMEM((rows, MIX_B), F32),
                        pltpu.VMEM((N_HEADS * WINDOW, 2 * WINDOW), F32)],
        compiler_params=_cparams(1),
        name="mix_prompt",
    )(x, gmix, win, lng, lnb, wm, bs, bkt, relb, sink, ga, gb, woa, wob)


def _mix_sample(x, ck, cv, wts, wm, bs, bkt, relb, sink, *, nseq, seq_len):
    gmix, win, lng, lnb, ga, gb, woa, wob = wts
    t = x.shape[0]
    rows = nseq * seq_len
    row_spec = lambda w: pl.BlockSpec((rows, w), lambda i: (i, 0))
    seq_spec = pl.BlockSpec((nseq, WINDOW, LANES), lambda i: (i, 0, 0))
    kern = functools.partial(_mix_sample_kernel, nseq=nseq, seq_len=seq_len)
    return pl.pallas_call(
        kern,
        grid=(t // rows,),
        in_specs=[row_spec(D_MODEL), seq_spec, seq_spec, _full(gmix.shape), _full(win.shape), _full(lng.shape),
                  _full(lnb.shape), _full(wm.shape), _full(bs.shape), _full(bkt.shape), _full(relb.shape),
                  _full(sink.shape),
                  _full(ga.shape), _full(gb.shape), _full(woa.shape), _full(wob.shape)],
        out_specs=[row_spec(D_MODEL), seq_spec, seq_spec, row_spec(MIX_A)],
        out_shape=[jax.ShapeDtypeStruct((t, D_MODEL), F32),
                   jax.ShapeDtypeStruct(ck.shape, F32), jax.ShapeDtypeStruct(cv.shape, F32),
                   jax.ShapeDtypeStruct((t, MIX_A), F32)],
        scratch_shapes=[pltpu.VMEM((nseq, 2 * WINDOW, LANES), F32), pltpu.VMEM((nseq, 2 * WINDOW, LANES), F32),
                        pltpu.VMEM((N_HEADS * seq_len, 2 * WINDOW), F32)],
        compiler_params=_cparams(1),
        name="mix_sample",
    )(x, ck, cv, gmix, win, lng, lnb, wm, bs, bkt, relb, sink, ga, gb, woa, wob)


def _mem_kv(mem, g, wk, wv):
    b, m, d = mem.shape
    return pl.pallas_call(
        _mem_kv_kernel,
        grid=(b,),
        in_specs=[pl.BlockSpec((1, m, d), lambda i: (i, 0, 0)), _full(g.shape), _full(wk.shape), _full(wv.shape)],
        out_specs=[pl.BlockSpec((1, m, MEM_INNER), lambda i: (i, 0, 0))] * 2,
        out_shape=[jax.ShapeDtypeStruct((b, m, MEM_INNER), F32)] * 2,
        compiler_params=_cparams(1),
        name="mem_kv",
    )(mem, g, wk, wv)


def _mem_outs(t, rows):
    row_spec = lambda w: pl.BlockSpec((rows, w), lambda i: (i, 0))
    specs = [row_spec(D_MODEL), row_spec(D_MODEL),
             pl.BlockSpec((2 * PEER_HEADS, rows, PEER_HALF), lambda i: (0, i, 0))]
    shapes = [jax.ShapeDtypeStruct((t, D_MODEL), F32), jax.ShapeDtypeStruct((t, D_MODEL), BF16),
              jax.ShapeDtypeStruct((2 * PEER_HEADS, t, PEER_HALF), BF16)]
    return specs, shapes


def _mem_prompt(h1, mk, mv, mwts, *, rows, seq):
    t = h1.shape[0]
    tiles_per_seq = seq // rows
    out_specs, out_shapes = _mem_outs(t, rows)
    kv_spec = pl.BlockSpec((1, N_MEM, MEM_INNER), lambda i: (i // tiles_per_seq, 0, 0))
    return pl.pallas_call(
        _mem_prompt_kernel,
        grid=(t // rows,),
        in_specs=[pl.BlockSpec((rows, D_MODEL), lambda i: (i, 0)), kv_spec, kv_spec] + [_full(w.shape) for w in mwts],
        out_specs=out_specs,
        out_shape=out_shapes,
        compiler_params=_cparams(1),
        name="mem_prompt",
    )(h1, mk, mv, *mwts)


def _mem_sample(h1, ck, cv, mwts, *, nseq, seq_len):
    t = h1.shape[0]
    rows = nseq * seq_len
    out_specs, out_shapes = _mem_outs(t, rows)
    kv_spec = pl.BlockSpec((nseq, N_MEM * MEM_HEADS, MEM_HEAD_DIM), lambda i: (i, 0, 0))
    kern = functools.partial(_mem_sample_kernel, nseq=nseq, seq_len=seq_len)
    return pl.pallas_call(
        kern,
        grid=(t // rows,),
        in_specs=[pl.BlockSpec((rows, D_MODEL), lambda i: (i, 0)), kv_spec, kv_spec] + [_full(w.shape) for w in mwts],
        out_specs=out_specs,
        out_shape=out_shapes,
        compiler_params=_cparams(1),
        name="mem_sample",
    )(h1, ck, cv, *mwts)


def _peer_prep(qp, keys, *, tokens):
    t = qp.shape[1]
    out_spec = pl.BlockSpec((PEER_HEADS, PEER_KEYS, tokens), lambda i: (0, 0, i))
    kern = functools.partial(_peer_prep_kernel, lane_tiles=tokens // LANES)
    return pl.pallas_call(
        kern,
        grid=(t // tokens,),
        in_specs=[pl.BlockSpec((2 * PEER_HEADS, tokens, PEER_HALF), lambda i: (0, i, 0)), _full(keys.shape)],
        out_specs=[out_spec] * 4,
        out_shape=[jax.ShapeDtypeStruct((PEER_HEADS, PEER_KEYS, t), dt) for dt in (BF16, BF16, F32, F32)],
        compiler_params=_cparams(1),
        name="peer_prep",
    )(qp, keys)


def _peer_ffn(xn, h2, eu, evt, prep, gfin, *, tokens, i1_per_step):
    t = xn.shape[0]
    experts = i1_per_step * PEER_KEYS
    tok_spec = pl.BlockSpec((tokens, D_MODEL), lambda i, e: (i, 0))
    prep_spec = pl.BlockSpec((PEER_HEADS, PEER_KEYS, tokens), lambda i, e: (0, 0, i))
    kern = functools.partial(_peer_ffn_kernel, tokens=tokens, i1_per_step=i1_per_step)
    return pl.pallas_call(
        kern,
        grid=(t // tokens, PEER_N // experts),
        in_specs=[tok_spec, tok_spec,
                  pl.BlockSpec((experts, D_MODEL), lambda i, e: (e, 0)),
                  pl.BlockSpec((D_MODEL, experts), lambda i, e: (0, e)),
                  prep_spec, prep_spec, prep_spec, prep_spec,
                  pl.BlockSpec(gfin.shape, lambda i, e: (0, 0))],
        out_specs=tok_spec,
        out_shape=jax.ShapeDtypeStruct((t, D_MODEL), F32),
        scratch_shapes=[pltpu.VMEM((D_MODEL, tokens), F32),
                        pltpu.VMEM((experts // 2, tokens), F32), pltpu.VMEM((experts // 2, tokens), F32),
                        pltpu.VMEM((experts // 2, tokens), BF16), pltpu.VMEM((experts // 2, tokens), BF16)],
        compiler_params=_cparams(2),
        name="peer_ffn",
    )(xn, h2, eu, evt, *prep, gfin)


def kernel(x_prompt, x_sample, mem_prompt, cache_swa_k, cache_swa_v, cache_mem_k, cache_mem_v, norm_mix_g, w_in, ln_v_g, ln_v_b, spatial_w, spatial_b, attn_sinks, rel_bias, norm_a_out_g, norm_b_out_g, w_out, norm_mem_g, norm_memsrc_g, w_mq, w_mk, w_mv, w_mo, norm_ffn_g, peer_wq, peer_keys, peer_u, peer_v, norm_final_g):
    batch, seq, _ = x_prompt.shape
    nsamp, dec_len, _ = x_sample.shape
    l = 0

    wts = _mix_weights(norm_mix_g[l], w_in[l], ln_v_g[l], ln_v_b[l],
                       norm_a_out_g[l], norm_b_out_g[l], w_out[l])
    tril = jnp.tril(jnp.ones((CHUNK, CHUNK), F32))
    wm = spatial_w[l] * tril
    wm_p = wm.astype(BF16)
    bs_p = spatial_b[l][:, :, None]
    seqs_per_chunk = CHUNK // dec_len
    eye = jnp.eye(seqs_per_chunk, dtype=F32)
    wm_s = jnp.einsum('ab,gts->gatbs', eye, wm[:, :dec_len, :dec_len]).reshape(A_GROUPS, CHUNK, CHUNK).astype(BF16)
    bs_s = jnp.tile(spatial_b[l][:, :dec_len], (1, seqs_per_chunk))[:, :, None]
    bkt_p = _bucket_table(WINDOW)
    bkt_s = _bucket_table(dec_len)
    sink_p = jnp.repeat(attn_sinks[l], WINDOW)[:, None]
    sink_s = jnp.repeat(attn_sinks[l], dec_len)[:, None]
    mwts = (norm_mem_g[l][None], w_mq[l].astype(BF16), w_mo[l].astype(BF16), norm_ffn_g[l][None],
            peer_wq[l].astype(BF16))
    keys = peer_keys[l].astype(BF16)
    eu = peer_u[l].astype(BF16)
    evt = peer_v[l].T.astype(BF16)
    gfin = norm_final_g[None]

    xp = x_prompt.reshape(batch * seq, D_MODEL)
    h1p, kp, vp = _mix_prompt(xp, wts, wm_p, bs_p, bkt_p, rel_bias, sink_p, rows=512, seq=seq)
    mk, mv = _mem_kv(mem_prompt, norm_memsrc_g[l][None], w_mk[l].astype(BF16), w_mv[l].astype(BF16))
    h2p, xnp_, qpp = _mem_prompt(h1p, mk, mv, mwts, rows=512, seq=seq)
    prep_p = _peer_prep(qpp, keys, tokens=512)
    yp = _peer_ffn(xnp_, h2p, eu, evt, prep_p, gfin, tokens=512, i1_per_step=16)

    xs = x_sample.reshape(nsamp * dec_len, D_MODEL)
    ck = cache_swa_k.reshape(nsamp, WINDOW, LANES)
    cv = cache_swa_v.reshape(nsamp, WINDOW, LANES)
    h1s, ks, vs, chunk_v = _mix_sample(xs, ck, cv, wts, wm_s, bs_s, bkt_s, rel_bias, sink_s,
                                       nseq=seqs_per_chunk, seq_len=dec_len)
    cmk = cache_mem_k.reshape(nsamp, N_MEM * MEM_HEADS, MEM_HEAD_DIM)
    cmv = cache_mem_v.reshape(nsamp, N_MEM * MEM_HEADS, MEM_HEAD_DIM)
    h2s, xns, qps = _mem_sample(h1s, cmk, cmv, mwts, nseq=8, seq_len=dec_len)
    prep_s = _peer_prep(qps, keys, tokens=512)
    ys = _peer_ffn(xns, h2s, eu, evt, prep_s, gfin, tokens=512, i1_per_step=16)

    kp4 = kp.reshape(batch, seq, LANES)[:, -WINDOW:].reshape(batch, WINDOW, N_KV, HEAD_DIM)
    vp4 = vp.reshape(batch, seq, LANES)[:, -WINDOW:].reshape(batch, WINDOW, N_KV, HEAD_DIM)
    return (yp.reshape(batch, seq, D_MODEL),
            ys.reshape(nsamp, dec_len, D_MODEL),
            kp4[None], vp4[None],
            mk.reshape(batch, N_MEM, MEM_HEADS, MEM_HEAD_DIM)[None],
            mv.reshape(batch, N_MEM, MEM_HEADS, MEM_HEAD_DIM)[None],
            ks.reshape(nsamp, WINDOW, N_KV, HEAD_DIM)[None],
            vs.reshape(nsamp, WINDOW, N_KV, HEAD_DIM)[None],
            chunk_v.reshape(nsamp, dec_len, A_GROUPS, A_GW)[None])
```

```python
import functools
import math

import jax
import jax.numpy as jnp
import numpy as np
from jax import lax
from jax.experimental import pallas as pl
from jax.experimental.pallas import tpu as pltpu

F32 = jnp.float32
BF16 = jnp.bfloat16

D_MODEL = 1024
MIX_A = 512
A_GROUPS = 4
A_GW = 128
CHUNK = 128
N_HEADS = 8
N_KV = 2
HEAD_DIM = 64
GQA = 4
MIX_B = 512
WINDOW = 128
IN_COLS = 1792
REL_BUCKETS = 32
REL_MAX_EXACT = 16
REL_MAX_DIST = 128
N_MEM = 256
MEM_HEADS = 4
MEM_HEAD_DIM = 128
MEM_INNER = 512
PEER_HEADS = 8
PEER_KEYS = 128
PEER_N = PEER_KEYS * PEER_KEYS
PEER_HALF = 128
PEER_TOPK = 16
EPS = 1e-6
NEG = -1e30

LANES = 128
SUBLANES = 8
VMEM_LIMIT = 56 * 1024 * 1024

NT_DIMS = (((1,), (1,)), ((), ()))


def _cparams(n_axes):
    return pltpu.CompilerParams(dimension_semantics=("arbitrary",) * n_axes, vmem_limit_bytes=VMEM_LIMIT)


def _rms(x, g):
    return x * lax.rsqrt(jnp.mean(x * x, axis=-1, keepdims=True) + EPS) * g


def _gelu(x):
    return 0.5 * x * (1.0 + jnp.tanh(math.sqrt(2.0 / math.pi) * (x + 0.044715 * (x * x * x))))


def _dot(a, b):
    return jnp.dot(a, b, preferred_element_type=F32)


def _dot_nt(a, b):
    return lax.dot_general(a, b, NT_DIMS, preferred_element_type=F32)


def _mix_in(x, gmix, win, lng, lnb):
    xn = _rms(x, gmix).astype(BF16)
    h = _dot(xn, win)
    z = _gelu(h[:, :2 * MIX_A])
    u = z[:, :MIX_A]
    vs = []
    for g in range(A_GROUPS):
        seg = z[:, MIX_A + g * A_GW:MIX_A + (g + 1) * A_GW]
        mu = jnp.mean(seg, axis=-1, keepdims=True)
        cen = seg - mu
        var = jnp.mean(cen * cen, axis=-1, keepdims=True)
        vs.append(cen * lax.rsqrt(var + EPS) * lng[:, g * A_GW:(g + 1) * A_GW] + lnb[:, g * A_GW:(g + 1) * A_GW])
    v = jnp.concatenate(vs, axis=1)
    r0 = 2 * MIX_A
    q = h[:, r0:r0 + MIX_B]
    k = h[:, r0 + MIX_B:r0 + MIX_B + N_KV * HEAD_DIM]
    vb = h[:, r0 + MIX_B + N_KV * HEAD_DIM:]
    return u, v, q, k, vb


def _swa_softmax(s, mask, sk):
    s = jnp.where(mask, s, NEG)
    m = jnp.maximum(jnp.max(s, axis=-1, keepdims=True), sk)
    p = jnp.exp(s - m)
    return p / (jnp.sum(p, axis=-1, keepdims=True) + jnp.exp(sk - m))


def _fill_rel_bias(bias_scr, bkt_ref, relb_ref):
    bkt = bkt_ref[...]
    n_q = bkt.shape[0]
    for h in range(N_HEADS):
        acc = jnp.zeros(bkt.shape, F32)
        for b in range(REL_BUCKETS):
            acc = jnp.where(bkt == b, relb_ref[b:b + 1, h:h + 1], acc)
        bias_scr[h * n_q:(h + 1) * n_q, :] = acc


def _mix_prompt_kernel(x_ref, gmix_ref, win_ref, lng_ref, lnb_ref, wm_ref, bs_ref, bkt_ref, relb_ref, sink_ref,
                       ga_ref, gb_ref, woa_ref, wob_ref, h1_ref, k_ref, v_ref,
                       kprev, vprev, ya_scr, ob_scr, bias_ref, *, rows, tiles_per_seq):
    i = pl.program_id(0)

    @pl.when(i == 0)
    def _():
        _fill_rel_bias(bias_ref, bkt_ref, relb_ref)

    x = x_ref[...]
    u, v, q, k, vb = _mix_in(x, gmix_ref[...], win_ref[...], lng_ref[...], lnb_ref[...])
    k_ref[...] = k
    v_ref[...] = vb
    first_tile = (i % tiles_per_seq) == 0

    @pl.when(first_tile)
    def _():
        kprev[...] = jnp.zeros_like(kprev)
        vprev[...] = jnp.zeros_like(vprev)

    nq = N_HEADS * WINDOW
    t_idx = lax.broadcasted_iota(jnp.int32, (nq, 2 * WINDOW), 0) % WINDOW
    kj = lax.broadcasted_iota(jnp.int32, (nq, 2 * WINDOW), 1)
    band = (kj > t_idx) & (kj <= t_idx + WINDOW)
    kj_min = jnp.where(first_tile, WINDOW, 0)
    band_first = band & (kj >= kj_min)
    lane = lax.broadcasted_iota(jnp.int32, (1, LANES), 1)
    lane_c = [lane < HEAD_DIM, lane >= HEAD_DIM]
    sk = sink_ref[...]
    bias = bias_ref[...]

    nblk = rows // WINDOW

    def block_rows(j):
        return slice(j * WINDOW, (j + 1) * WINDOW)

    def scores(j):
        r = block_rows(j)
        k_prev = kprev[...] if j == 0 else k[block_rows(j - 1)]
        kcat = jnp.concatenate([k_prev, k[r]], axis=0).astype(BF16)
        qg = jnp.concatenate([q[r, g * LANES:(g + 1) * LANES] for g in range(GQA)], axis=0)
        qs = jnp.concatenate([jnp.where(lane_c[c], qg, 0.0) for c in range(N_KV)], axis=0).astype(BF16)
        return _dot_nt(qs, kcat) * (1.0 / math.sqrt(HEAD_DIM)) + bias

    s_next = scores(0)
    for j in range(nblk):
        r = block_rows(j)
        s = s_next
        if j + 1 < nblk:
            s_next = scores(j + 1)
        for g in range(A_GROUPS):
            c = slice(g * A_GW, (g + 1) * A_GW)
            gate = _dot(wm_ref[g], v[r, c].astype(BF16)) + bs_ref[g]
            ya_scr[r, c] = u[r, c] * gate
        v_prev = vprev[...] if j == 0 else vb[block_rows(j - 1)]
        vcat = jnp.concatenate([v_prev, vb[r]], axis=0)
        p = _swa_softmax(s, band_first if j == 0 else band, sk).astype(BF16)
        half = GQA * WINDOW
        o = (_dot(p[:half], jnp.where(lane_c[0], vcat, 0.0).astype(BF16))
             + _dot(p[half:], jnp.where(lane_c[1], vcat, 0.0).astype(BF16)))
        for g in range(GQA):
            ob_scr[r, g * LANES:(g + 1) * LANES] = o[g * WINDOW:(g + 1) * WINDOW]

    kprev[...] = k[rows - WINDOW:]
    vprev[...] = vb[rows - WINDOW:]
    ya = _rms(ya_scr[...], ga_ref[...]).astype(BF16)
    ob = _rms(ob_scr[...], gb_ref[...]).astype(BF16)
    h1_ref[...] = x + _dot(ya, woa_ref[...]) + _dot(ob, wob_ref[...])


def _mix_sample_kernel(x_ref, ck_ref, cv_ref, gmix_ref, win_ref, lng_ref, lnb_ref, wm_ref, bs_ref, bkt_ref,
                       relb_ref, sink_ref, ga_ref, gb_ref, woa_ref, wob_ref,
                       h1_ref, ko_ref, vo_ref, cvout_ref, kk_scr, vv_scr, bias_ref, *, nseq, seq_len):
    i = pl.program_id(0)
    wbuf = WINDOW

    @pl.when(i == 0)
    def _():
        kk_scr[...] = jnp.zeros_like(kk_scr)
        vv_scr[...] = jnp.zeros_like(vv_scr)
        _fill_rel_bias(bias_ref, bkt_ref, relb_ref)

    x = x_ref[...]
    u, v, q, k, vb = _mix_in(x, gmix_ref[...], win_ref[...], lng_ref[...], lnb_ref[...])
    cvout_ref[...] = v
    ya_parts = []
    for g in range(A_GROUPS):
        c = slice(g * A_GW, (g + 1) * A_GW)
        s = _dot(wm_ref[g], v[:, c].astype(BF16)) + bs_ref[g]
        ya_parts.append(u[:, c] * s)
    ya = jnp.concatenate(ya_parts, axis=1)

    kk_scr[:, 0:wbuf, :] = ck_ref[...]
    vv_scr[:, 0:wbuf, :] = cv_ref[...]
    kk_scr[:, wbuf:wbuf + seq_len, :] = k.reshape(nseq, seq_len, LANES)
    vv_scr[:, wbuf:wbuf + seq_len, :] = vb.reshape(nseq, seq_len, LANES)
    ko_ref[...] = kk_scr[:, seq_len:wbuf + seq_len, :]
    vo_ref[...] = vv_scr[:, seq_len:wbuf + seq_len, :]

    nq = N_HEADS * seq_len
    l_idx = lax.broadcasted_iota(jnp.int32, (nq, 2 * WINDOW), 0) % seq_len
    kj = lax.broadcasted_iota(jnp.int32, (nq, 2 * WINDOW), 1)
    band = (kj > l_idx) & (kj <= l_idx + wbuf)
    lane = lax.broadcasted_iota(jnp.int32, (1, 1, LANES), 2)
    lane_c = [lane < HEAD_DIM, lane >= HEAD_DIM]

    q3 = q.reshape(nseq, seq_len, MIX_B)
    qg = jnp.concatenate([q3[:, :, g * LANES:(g + 1) * LANES] for g in range(GQA)], axis=1)
    qs = jnp.concatenate([jnp.where(lane_c[c], qg, 0.0) for c in range(N_KV)], axis=1).astype(BF16)
    kk = kk_scr[...].astype(BF16)
    vv = vv_scr[...]
    s = jnp.einsum('nqe,nke->nqk', qs, kk, preferred_element_type=F32) * (1.0 / math.sqrt(HEAD_DIM))
    s = s + bias_ref[...][None]
    p = _swa_softmax(s, band[None], sink_ref[...][None]).astype(BF16)
    half = GQA * seq_len
    o = (jnp.einsum('nqk,nke->nqe', p[:, :half], jnp.where(lane_c[0], vv, 0.0).astype(BF16),
                    preferred_element_type=F32)
         + jnp.einsum('nqk,nke->nqe', p[:, half:], jnp.where(lane_c[1], vv, 0.0).astype(BF16),
                      preferred_element_type=F32))
    ob3 = jnp.concatenate([o[:, g * seq_len:(g + 1) * seq_len, :] for g in range(GQA)], axis=2)
    ob = ob3.reshape(nseq * seq_len, MIX_B)
    ya_n = _rms(ya, ga_ref[...]).astype(BF16)
    ob_n = _rms(ob, gb_ref[...]).astype(BF16)
    h1_ref[...] = x + _dot(ya_n, woa_ref[...]) + _dot(ob_n, wob_ref[...])


def _mem_kv_kernel(mem_ref, g_ref, wk_ref, wv_ref, mk_ref, mv_ref):
    mn = _rms(mem_ref[0], g_ref[...]).astype(BF16)
    mk_ref[0] = _dot(mn, wk_ref[...])
    mv_ref[0] = _dot(mn, wv_ref[...])


def _mem_tail(h1, o, wmo_ref, gffn_ref, wq_ref, h2_ref, xn_ref, qp_ref):
    h2 = h1 + _dot(o.astype(BF16), wmo_ref[...])
    h2_ref[...] = h2
    xn = _rms(h2, gffn_ref[...]).astype(BF16)
    xn_ref[...] = xn
    qp = _dot(xn, wq_ref[...]).astype(BF16)
    for hp in range(2 * PEER_HEADS):
        qp_ref[hp] = qp[:, hp * PEER_HALF:(hp + 1) * PEER_HALF]


def _mem_prompt_kernel(h1_ref, mk_ref, mv_ref, gmem_ref, wmq_ref, wmo_ref, gffn_ref, wq_ref,
                       h2_ref, xn_ref, qp_ref):
    h1 = h1_ref[...]
    q = _dot(_rms(h1, gmem_ref[...]).astype(BF16), wmq_ref[...]).astype(BF16)
    mk = mk_ref[0].astype(BF16)
    mv = mv_ref[0].astype(BF16)
    def head_cols(hh):
        return slice(hh * MEM_HEAD_DIM, (hh + 1) * MEM_HEAD_DIM)

    def scores(hh):
        c = head_cols(hh)
        return _dot_nt(q[:, c], mk[:, c]) * (1.0 / math.sqrt(MEM_HEAD_DIM))

    outs = []
    s_next = scores(0)
    for hh in range(MEM_HEADS):
        c = head_cols(hh)
        s = s_next
        if hh + 1 < MEM_HEADS:
            s_next = scores(hh + 1)
        m = jnp.max(s, axis=-1, keepdims=True)
        p = jnp.exp(s - m)
        p = (p / jnp.sum(p, axis=-1, keepdims=True)).astype(BF16)
        outs.append(_dot(p, mv[:, c]))
    o = jnp.concatenate(outs, axis=1)
    _mem_tail(h1, o, wmo_ref, gffn_ref, wq_ref, h2_ref, xn_ref, qp_ref)


def _mem_sample_kernel(h1_ref, ck_ref, cv_ref, gmem_ref, wmq_ref, wmo_ref, gffn_ref, wq_ref,
                       h2_ref, xn_ref, qp_ref, *, nseq, seq_len):
    h1 = h1_ref[...]
    q = _dot(_rms(h1, gmem_ref[...]).astype(BF16), wmq_ref[...])
    q3 = q.reshape(nseq, seq_len, MEM_INNER)
    qs = jnp.concatenate([q3[:, :, hh * MEM_HEAD_DIM:(hh + 1) * MEM_HEAD_DIM] for hh in range(MEM_HEADS)],
                         axis=1).astype(BF16)
    ck = ck_ref[...].astype(BF16)
    cv = cv_ref[...].astype(BF16)
    s = jnp.einsum('nqd,nkd->nqk', qs, ck, preferred_element_type=F32) * (1.0 / math.sqrt(MEM_HEAD_DIM))
    nq, nk = MEM_HEADS * seq_len, N_MEM * MEM_HEADS
    row_head = lax.broadcasted_iota(jnp.int32, (nq, nk), 0) // seq_len
    col_head = lax.broadcasted_iota(jnp.int32, (nq, nk), 1) % MEM_HEADS
    s = jnp.where((row_head == col_head)[None], s, -jnp.inf)
    m = jnp.max(s, axis=-1, keepdims=True)
    p = jnp.exp(s - m)
    p = (p / jnp.sum(p, axis=-1, keepdims=True)).astype(BF16)
    of = jnp.einsum('nqk,nkd->nqd', p, cv, preferred_element_type=F32)
    o3 = jnp.concatenate([of[:, hh * seq_len:(hh + 1) * seq_len] for hh in range(MEM_HEADS)], axis=2)
    o = o3.reshape(nseq * seq_len, MEM_INNER)
    _mem_tail(h1, o, wmo_ref, gffn_ref, wq_ref, h2_ref, xn_ref, qp_ref)


def _sort_network(n):
    out, p = [], 1
    while p < n:
        k = p
        while k >= 1:
            for j in range(k % p, n - k, 2 * k):
                for i in range(min(k, n - j - k)):
                    if (i + j) // (2 * p) == (i + j + k) // (2 * p):
                        out.append((i + j, i + j + k))
            k //= 2
        p *= 2
    return out


_SORT16 = _sort_network(PEER_TOPK)


def _compare_exchange(x, i, j):
    hi, lo = jnp.maximum(x[i], x[j]), jnp.minimum(x[i], x[j])
    x[i], x[j] = hi, lo


def _top16_sorted(slabs):
    n = len(slabs)
    x = list(slabs)
    for i, j in _SORT16:
        if j < n:
            _compare_exchange(x, i, j)
    for shift in (4, 2, 1):
        y = []
        for i in range(PEER_TOPK):
            lo = x[i] if i < n else None
            k = PEER_TOPK - 1 - i
            hi = pltpu.roll(x[k], shift, axis=0) if k < n else None
            y.append(hi if lo is None else lo if hi is None else jnp.maximum(lo, hi))
        d = PEER_TOPK // 2
        while d >= 1:
            for i in range(PEER_TOPK):
                if i & d == 0:
                    _compare_exchange(y, i, i + d)
            d //= 2
        x, n = y, PEER_TOPK
    return x


def _step_count(slabs, thresholds, below):
    out = []
    for x in slabs:
        c = jnp.zeros_like(x)
        for j, t in enumerate(thresholds):
            c = jnp.where((x < t) if below else (x >= t), float(j + 1), c)
        out.append(c)
    return out


def _peer_prep_kernel(qp_ref, keys_ref, r2_ref, e2_ref, cnt_ref, e1_ref, *, lane_tiles):
    nslab = PEER_KEYS // SUBLANES

    def per_head(h, carry):
        for lt in range(lane_tiles):
            lanes = slice(lt * LANES, (lt + 1) * LANES)
            q1 = qp_ref[2 * h, lanes, :]
            q2 = qp_ref[2 * h + 1, lanes, :]
            s1 = _dot_nt(keys_ref[h, 0], q1)
            s2 = _dot_nt(keys_ref[h, 1], q2)
            s1s = [s1[r * SUBLANES:(r + 1) * SUBLANES] for r in range(nslab)]
            s2s = [s2[r * SUBLANES:(r + 1) * SUBLANES] for r in range(nslab)]
            a = _top16_sorted(s1s)
            b = _top16_sorted(s2s)
            sub = lax.broadcasted_iota(jnp.int32, a[0].shape, 0)

            def by_sublane(vals):
                out = vals[SUBLANES - 1]
                for r in range(SUBLANES - 2, -1, -1):
                    out = jnp.where(sub == r, vals[r], out)
                return out

            a_lo, a_hi = by_sublane(a[:8]), by_sublane(a[8:])
            b_lo, b_hi = by_sublane(b[:8]), by_sublane(b[8:])
            tail = lambda x: jnp.where(sub >= 3, x, -jnp.inf)
            cand = [a_lo + b[0], a_hi + b[0], a_lo + b[1], a_lo + b[2], tail(a[0] + b_lo), a[0] + b_hi,
                    tail(a[1] + b_lo), tail(a[2] + b_lo), tail(a[3] + b_lo)]
            tops = _top16_sorted(cand)
            tau = tops[-1]
            z = jnp.ones_like(tau)
            for t in tops[1:]:
                z = z + jnp.exp(t - tops[0])
            thr = []
            for j in range(PEER_TOPK):
                t = jnp.full_like(tau, jnp.inf)
                for i in range(PEER_TOPK // (j + 1)):
                    t = jnp.where(a[i] + b[j] >= tau, a[i], t)
                thr.append(t)
            cnt = _step_count(s1s, thr, below=False)
            rank2 = _step_count(s2s, b, below=True)
            half_inv_z = 0.5 / z
            r2_ref[h, :, lanes] = jnp.concatenate(rank2, axis=0).astype(BF16)
            e2_ref[h, :, lanes] = jnp.concatenate([jnp.exp(x - b[0]) * half_inv_z for x in s2s], axis=0).astype(BF16)
            cnt_ref[h, :, lanes] = jnp.concatenate(cnt, axis=0)
            e1_ref[h, :, lanes] = jnp.concatenate([jnp.exp(x - a[0]) for x in s1s], axis=0)
        return carry

    lax.fori_loop(0, PEER_HEADS, per_head, 0)


def _peer_weight_rows(j, row, lanes, cnt_slab, e1_slab, x_ref, p_ref, r2_ref, e2_ref):
    sub, width = SUBLANES, lanes.stop - lanes.start
    rows = slice(j * PEER_KEYS, (j + 1) * PEER_KEYS)
    x = x_ref[rows, lanes]
    c0 = math.sqrt(2.0 / math.pi)
    act = (x + x * jnp.tanh(x * ((0.044715 * c0) * (x * x) + c0))).astype(BF16)
    w = None
    for h in range(PEER_HEADS):
        cnt = jnp.broadcast_to(cnt_slab[h][row:row + 1, lanes], (sub, width)).astype(BF16)
        e1 = jnp.broadcast_to(e1_slab[h][row:row + 1, lanes], (sub, width)).astype(BF16)
        cnt = jnp.concatenate([cnt] * (PEER_KEYS // sub), axis=0)
        e1 = jnp.concatenate([e1] * (PEER_KEYS // sub), axis=0)
        e2 = e2_ref[h, :, lanes]
        term = jnp.where(r2_ref[h, :, lanes] < cnt, e2, jnp.zeros_like(e2)) * e1
        w = term if w is None else w + term
    p_ref[rows, lanes] = w * act


def _interleave(mxu_chunks, vpu_chunks):
    per = -(-len(vpu_chunks) // max(len(mxu_chunks), 1))
    for c, mxu in enumerate(mxu_chunks):
        mxu()
        for vpu in vpu_chunks[c * per:(c + 1) * per]:
            vpu()
    for vpu in vpu_chunks[len(mxu_chunks) * per:]:
        vpu()


def _peer_ffn_kernel(xn_ref, h2_ref, eu_ref, evt_ref, r2_ref, e2_ref, cnt_ref, e1_ref, gfin_ref,
                     y_ref, acc_ref, x_a, x_b, p_a, p_b, *, tokens, i1_per_step, n_parts):
    e = pl.program_id(1)
    part_keys = i1_per_step // n_parts
    part = part_keys * PEER_KEYS
    width = 2 * LANES
    n_lt = tokens // width
    d_rows = D_MODEL // 2
    xs, ps = (x_a, x_b), (p_a, p_b)

    @pl.when(e == 0)
    def _():
        acc_ref[...] = jnp.zeros_like(acc_ref)

    i1_0 = pl.multiple_of(e * i1_per_step, i1_per_step)
    cnt_slab = [cnt_ref[h, pl.ds(i1_0, i1_per_step), :] for h in range(PEER_HEADS)]
    e1_slab = [e1_ref[h, pl.ds(i1_0, i1_per_step), :] for h in range(PEER_HEADS)]

    def u_product(t, lt):
        lanes = slice(lt * width, (lt + 1) * width)
        xs[t % 2][:, lanes] = _dot_nt(eu_ref[t * part:(t + 1) * part, :], xn_ref[lanes, :])

    def v_product(t, dh, lt):
        rows = slice(dh * d_rows, (dh + 1) * d_rows)
        lanes = slice(lt * width, (lt + 1) * width)
        acc_ref[rows, lanes] += _dot(evt_ref[rows, t * part:(t + 1) * part], ps[t % 2][:, lanes])

    def weights(t):
        return [functools.partial(_peer_weight_rows, j, t * part_keys + j, slice(lt * width, (lt + 1) * width),
                                  cnt_slab, e1_slab, xs[t % 2], ps[t % 2], r2_ref, e2_ref)
                for lt in range(n_lt) for j in range(part_keys)]

    for t in range(n_parts + 2):
        mxu = []
        if t < n_parts:
            mxu += [functools.partial(u_product, t, lt) for lt in range(n_lt)]
        if t >= 2:
            mxu += [functools.partial(v_product, t - 2, dh, lt) for lt in range(n_lt) for dh in range(2)]
        _interleave(mxu, weights(t - 1) if 1 <= t <= n_parts else [])

    @pl.when(e == pl.num_programs(1) - 1)
    def _():
        y_ref[...] = _rms(h2_ref[...] + acc_ref[...].T, gfin_ref[...])


def _full(shape):
    return pl.BlockSpec(shape, lambda *_: (0,) * len(shape))


def _t5_bucket(d):
    n = np.maximum(d, 0)
    nf = np.maximum(n, 1).astype(np.float32)
    scaled = (np.log(nf / np.float32(REL_MAX_EXACT)) / np.float32(math.log(REL_MAX_DIST / REL_MAX_EXACT))
              * np.float32(REL_BUCKETS - REL_MAX_EXACT))
    frac = np.abs(scaled - np.round(scaled))
    capped = scaled > REL_BUCKETS - REL_MAX_EXACT
    assert np.all((frac > 1e-3) | (frac < 1e-5) | capped), "bucket boundary too close to an integer distance"
    large = np.minimum(REL_MAX_EXACT + np.floor(scaled + 1e-4).astype(np.int32), REL_BUCKETS - 1)
    return np.where(n < REL_MAX_EXACT, n, large).astype(np.int32)


def _bucket_table(n_q):
    d = (np.arange(n_q)[:, None] + WINDOW) - np.arange(2 * WINDOW)[None, :]
    return jnp.asarray(_t5_bucket(d))


def _head_perm():
    g, c, d = jnp.meshgrid(jnp.arange(GQA), jnp.arange(N_KV), jnp.arange(HEAD_DIM), indexing='ij')
    return ((c * GQA + g) * HEAD_DIM + d).reshape(-1)


def _mix_weights(norm_mix_g, w_in, ln_v_g, ln_v_b, norm_a_out_g, norm_b_out_g, w_out):
    perm = _head_perm()
    q0 = 2 * MIX_A
    win = jnp.concatenate([w_in[:, :q0], w_in[:, q0:q0 + MIX_B][:, perm], w_in[:, q0 + MIX_B:]], axis=1).astype(BF16)
    woa = w_out[:MIX_A].astype(BF16)
    wob = w_out[MIX_A:][perm].astype(BF16)
    gb = norm_b_out_g[perm][None]
    return (norm_mix_g[None], win, ln_v_g[None], ln_v_b[None], norm_a_out_g[None], gb, woa, wob)


def _mix_prompt(x, wts, wm, bs, bkt, relb, sink, *, rows, seq):
    gmix, win, lng, lnb, ga, gb, woa, wob = wts
    t = x.shape[0]
    row_spec = lambda w: pl.BlockSpec((rows, w), lambda i: (i, 0))
    kern = functools.partial(_mix_prompt_kernel, rows=rows, tiles_per_seq=seq // rows)
    return pl.pallas_call(
        kern,
        grid=(t // rows,),
        in_specs=[row_spec(D_MODEL), _full(gmix.shape), _full(win.shape), _full(lng.shape), _full(lnb.shape),
                  _full(wm.shape), _full(bs.shape), _full(bkt.shape), _full(relb.shape), _full(sink.shape),
                  _full(ga.shape), _full(gb.shape), _full(woa.shape), _full(wob.shape)],
        out_specs=[row_spec(D_MODEL), row_spec(LANES), row_spec(LANES)],
        out_shape=[jax.ShapeDtypeStruct((t, D_MODEL), F32), jax.ShapeDtypeStruct((t, LANES), F32),
                   jax.ShapeDtypeStruct((t, LANES), F32)],
        scratch_shapes=[pltpu.VMEM((WINDOW, LANES), F32), pltpu.VMEM((WINDOW, LANES), F32),
                        pltpu.VMEM((rows, MIX_A), F32), pltpu.VMEM((rows, MIX_B), F32),
                        pltpu.VMEM((N_HEADS * WINDOW, 2 * WINDOW), F32)],
        compiler_params=_cparams(1),
        name="mix_prompt",
    )(x, gmix, win, lng, lnb, wm, bs, bkt, relb, sink, ga, gb, woa, wob)


def _mix_sample(x, ck, cv, wts, wm, bs, bkt, relb, sink, *, nseq, seq_len):
    gmix, win, lng, lnb, ga, gb, woa, wob = wts
    t = x.shape[0]
    rows = nseq * seq_len
    row_spec = lambda w: pl.BlockSpec((rows, w), lambda i: (i, 0))
    seq_spec = pl.BlockSpec((nseq, WINDOW, LANES), lambda i: (i, 0, 0))
    kern = functools.partial(_mix_sample_kernel, nseq=nseq, seq_len=seq_len)
    return pl.pallas_call(
        kern,
        grid=(t // rows,),
        in_specs=[row_spec(D_MODEL), seq_spec, seq_spec, _full(gmix.shape), _full(win.shape), _full(lng.shape),
                  _full(lnb.shape), _full(wm.shape), _full(bs.shape), _full(bkt.shape), _full(relb.shape),
                  _full(sink.shape),
                  _full(ga.shape), _full(gb.shape), _full(woa.shape), _full(wob.shape)],
        out_specs=[row_spec(D_MODEL), seq_spec, seq_spec, row_spec(MIX_A)],
        out_shape=[jax.ShapeDtypeStruct((t, D_MODEL), F32),
                   jax.ShapeDtypeStruct(ck.shape, F32), jax.ShapeDtypeStruct(cv.shape, F32),
                   jax.ShapeDtypeStruct((t, MIX_A), F32)],
        scratch_shapes=[pltpu.VMEM((nseq, 2 * WINDOW, LANES), F32), pltpu.VMEM((nseq, 2 * WINDOW, LANES), F32),
                        pltpu.VMEM((N_HEADS * seq_len, 2 * WINDOW), F32)],
        compiler_params=_cparams(1),
        name="mix_sample",
    )(x, ck, cv, gmix, win, lng, lnb, wm, bs, bkt, relb, sink, ga, gb, woa, wob)


def _mem_kv(mem, g, wk, wv):
    b, m, d = mem.shape
    return pl.pallas_call(
        _mem_kv_kernel,
        grid=(b,),
        in_specs=[pl.BlockSpec((1, m, d), lambda i: (i, 0, 0)), _full(g.shape), _full(wk.shape), _full(wv.shape)],
        out_specs=[pl.BlockSpec((1, m, MEM_INNER), lambda i: (i, 0, 0))] * 2,
        out_shape=[jax.ShapeDtypeStruct((b, m, MEM_INNER), F32)] * 2,
        compiler_params=_cparams(1),
        name="mem_kv",
    )(mem, g, wk, wv)


def _mem_outs(t, rows):
    row_spec = lambda w: pl.BlockSpec((rows, w), lambda i: (i, 0))
    specs = [row_spec(D_MODEL), row_spec(D_MODEL),
             pl.BlockSpec((2 * PEER_HEADS, rows, PEER_HALF), lambda i: (0, i, 0))]
    shapes = [jax.ShapeDtypeStruct((t, D_MODEL), F32), jax.ShapeDtypeStruct((t, D_MODEL), BF16),
              jax.ShapeDtypeStruct((2 * PEER_HEADS, t, PEER_HALF), BF16)]
    return specs, shapes


def _mem_prompt(h1, mk, mv, mwts, *, rows, seq):
    t = h1.shape[0]
    tiles_per_seq = seq // rows
    out_specs, out_shapes = _mem_outs(t, rows)
    kv_spec = pl.BlockSpec((1, N_MEM, MEM_INNER), lambda i: (i // tiles_per_seq, 0, 0))
    return pl.pallas_call(
        _mem_prompt_kernel,
        grid=(t // rows,),
        in_specs=[pl.BlockSpec((rows, D_MODEL), lambda i: (i, 0)), kv_spec, kv_spec] + [_full(w.shape) for w in mwts],
        out_specs=out_specs,
        out_shape=out_shapes,
        compiler_params=_cparams(1),
        name="mem_prompt",
    )(h1, mk, mv, *mwts)


def _mem_sample(h1, ck, cv, mwts, *, nseq, seq_len):
    t = h1.shape[0]
    rows = nseq * seq_len
    out_specs, out_shapes = _mem_outs(t, rows)
    kv_spec = pl.BlockSpec((nseq, N_MEM * MEM_HEADS, MEM_HEAD_DIM), lambda i: (i, 0, 0))
    kern = functools.partial(_mem_sample_kernel, nseq=nseq, seq_len=seq_len)
    return pl.pallas_call(
        kern,
        grid=(t // rows,),
        in_specs=[pl.BlockSpec((rows, D_MODEL), lambda i: (i, 0)), kv_spec, kv_spec] + [_full(w.shape) for w in mwts],
        out_specs=out_specs,
        out_shape=out_shapes,
        compiler_params=_cparams(1),
        name="mem_sample",
    )(h1, ck, cv, *mwts)


def _peer_prep(qp, keys, *, tokens):
    t = qp.shape[1]
    out_spec = pl.BlockSpec((PEER_HEADS, PEER_KEYS, tokens), lambda i: (0, 0, i))
    kern = functools.partial(_peer_prep_kernel, lane_tiles=tokens // LANES)
    return pl.pallas_call(
        kern,
        grid=(t // tokens,),
        in_specs=[pl.BlockSpec((2 * PEER_HEADS, tokens, PEER_HALF), lambda i: (0, i, 0)), _full(keys.shape)],
        out_specs=[out_spec] * 4,
        out_shape=[jax.ShapeDtypeStruct((PEER_HEADS, PEER_KEYS, t), dt) for dt in (BF16, BF16, F32, F32)],
        compiler_params=_cparams(1),
        name="peer_prep",
    )(qp, keys)


def _peer_ffn(xn, h2, eu, evt, prep, gfin, *, tokens, i1_per_step):
    t = xn.shape[0]
    experts = i1_per_step * PEER_KEYS
    tok_spec = pl.BlockSpec((tokens, D_MODEL), lambda i, e: (i, 0))
    prep_spec = pl.BlockSpec((PEER_HEADS, PEER_KEYS, tokens), lambda i, e: (0, 0, i))
    n_parts = 4
    part = experts // n_parts
    kern = functools.partial(_peer_ffn_kernel, tokens=tokens, i1_per_step=i1_per_step, n_parts=n_parts)
    return pl.pallas_call(
        kern,
        grid=(t // tokens, PEER_N // experts),
        in_specs=[tok_spec, tok_spec,
                  pl.BlockSpec((experts, D_MODEL), lambda i, e: (e, 0)),
                  pl.BlockSpec((D_MODEL, experts), lambda i, e: (0, e)),
                  prep_spec, prep_spec, prep_spec, prep_spec,
                  pl.BlockSpec(gfin.shape, lambda i, e: (0, 0))],
        out_specs=tok_spec,
        out_shape=jax.ShapeDtypeStruct((t, D_MODEL), F32),
        scratch_shapes=[pltpu.VMEM((D_MODEL, tokens), F32),
                        pltpu.VMEM((part, tokens), F32), pltpu.VMEM((part, tokens), F32),
                        pltpu.VMEM((part, tokens), BF16), pltpu.VMEM((part, tokens), BF16)],
        compiler_params=_cparams(2),
        name="peer_ffn",
    )(xn, h2, eu, evt, *prep, gfin)


def kernel(x_prompt, x_sample, mem_prompt, cache_swa_k, cache_swa_v, cache_mem_k, cache_mem_v, norm_mix_g, w_in, ln_v_g, ln_v_b, spatial_w, spatial_b, attn_sinks, rel_bias, norm_a_out_g, norm_b_out_g, w_out, norm_mem_g, norm_memsrc_g, w_mq, w_mk, w_mv, w_mo, norm_ffn_g, peer_wq, peer_keys, peer_u, peer_v, norm_final_g):
    batch, seq, _ = x_prompt.shape
    nsamp, dec_len, _ = x_sample.shape
    l = 0

    wts = _mix_weights(norm_mix_g[l], w_in[l], ln_v_g[l], ln_v_b[l],
                       norm_a_out_g[l], norm_b_out_g[l], w_out[l])
    tril = jnp.tril(jnp.ones((CHUNK, CHUNK), F32))
    wm = spatial_w[l] * tril
    wm_p = wm.astype(BF16)
    bs_p = spatial_b[l][:, :, None]
    seqs_per_chunk = CHUNK // dec_len
    eye = jnp.eye(seqs_per_chunk, dtype=F32)
    wm_s = jnp.einsum('ab,gts->gatbs', eye, wm[:, :dec_len, :dec_len]).reshape(A_GROUPS, CHUNK, CHUNK).astype(BF16)
    bs_s = jnp.tile(spatial_b[l][:, :dec_len], (1, seqs_per_chunk))[:, :, None]
    bkt_p = _bucket_table(WINDOW)
    bkt_s = _bucket_table(dec_len)
    sink_p = jnp.repeat(attn_sinks[l], WINDOW)[:, None]
    sink_s = jnp.repeat(attn_sinks[l], dec_len)[:, None]
    mwts = (norm_mem_g[l][None], w_mq[l].astype(BF16), w_mo[l].astype(BF16), norm_ffn_g[l][None],
            peer_wq[l].astype(BF16))
    keys = peer_keys[l].astype(BF16)
    eu = peer_u[l].astype(BF16)
    evt = peer_v[l].T.astype(BF16)
    gfin = norm_final_g[None]

    xp = x_prompt.reshape(batch * seq, D_MODEL)
    h1p, kp, vp = _mix_prompt(xp, wts, wm_p, bs_p, bkt_p, rel_bias, sink_p, rows=512, seq=seq)
    mk, mv = _mem_kv(mem_prompt, norm_memsrc_g[l][None], w_mk[l].astype(BF16), w_mv[l].astype(BF16))
    h2p, xnp_, qpp = _mem_prompt(h1p, mk, mv, mwts, rows=512, seq=seq)
    prep_p = _peer_prep(qpp, keys, tokens=512)
    yp = _peer_ffn(xnp_, h2p, eu, evt, prep_p, gfin, tokens=512, i1_per_step=16)

    xs = x_sample.reshape(nsamp * dec_len, D_MODEL)
    ck = cache_swa_k.reshape(nsamp, WINDOW, LANES)
    cv = cache_swa_v.reshape(nsamp, WINDOW, LANES)
    h1s, ks, vs, chunk_v = _mix_sample(xs, ck, cv, wts, wm_s, bs_s, bkt_s, rel_bias, sink_s,
                                       nseq=seqs_per_chunk, seq_len=dec_len)
    cmk = cache_mem_k.reshape(nsamp, N_MEM * MEM_HEADS, MEM_HEAD_DIM)
    cmv = cache_mem_v.reshape(nsamp, N_MEM * MEM_HEADS, MEM_HEAD_DIM)
    h2s, xns, qps = _mem_sample(h1s, cmk, cmv, mwts, nseq=8, seq_len=dec_len)
    prep_s = _peer_prep(qps, keys, tokens=512)
    ys = _peer_ffn(xns, h2s, eu, evt, prep_s, gfin, tokens=512, i1_per_step=16)

    kp4 = kp.reshape(batch, seq, LANES)[:, -WINDOW:].reshape(batch, WINDOW, N_KV, HEAD_DIM)
    vp4 = vp.reshape(batch, seq, LANES)[:, -WINDOW:].reshape(batch, WINDOW, N_KV, HEAD_DIM)
    return (yp.reshape(batch, seq, D_MODEL),
            ys.reshape(nsamp, dec_len, D_MODEL),
            kp4[None], vp4[None],
            mk.reshape(batch, N_MEM, MEM_HEADS, MEM_HEAD_DIM)[None],
            mv.reshape(batch, N_MEM, MEM_HEADS, MEM_HEAD_DIM)[None],
            ks.reshape(nsamp, WINDOW, N_KV, HEAD_DIM)[None],
            vs.reshape(nsamp, WINDOW, N_KV, HEAD_DIM)[None],
            chunk_v.reshape(nsamp, dec_len, A_GROUPS, A_GW)[None])
```

```python
import functools
import math

import jax
import jax.numpy as jnp
import numpy as np
from jax import lax
from jax.experimental import pallas as pl
from jax.experimental.pallas import tpu as pltpu

F32 = jnp.float32
BF16 = jnp.bfloat16

D_MODEL = 1024
MIX_A = 512
A_GROUPS = 4
A_GW = 128
CHUNK = 128
N_HEADS = 8
N_KV = 2
HEAD_DIM = 64
GQA = 4
MIX_B = 512
WINDOW = 128
IN_COLS = 1792
REL_BUCKETS = 32
REL_MAX_EXACT = 16
REL_MAX_DIST = 128
N_MEM = 256
MEM_HEADS = 4
MEM_HEAD_DIM = 128
MEM_INNER = 512
PEER_HEADS = 8
PEER_KEYS = 128
PEER_N = PEER_KEYS * PEER_KEYS
PEER_HALF = 128
PEER_TOPK = 16
EPS = 1e-6
NEG = -1e30

LANES = 128
SUBLANES = 8
VMEM_LIMIT = 56 * 1024 * 1024

NT_DIMS = (((1,), (1,)), ((), ()))


def _cparams(n_axes):
    return pltpu.CompilerParams(dimension_semantics=("arbitrary",) * n_axes, vmem_limit_bytes=VMEM_LIMIT)


def _rms(x, g):
    return x * lax.rsqrt(jnp.mean(x * x, axis=-1, keepdims=True) + EPS) * g


def _gelu(x):
    return 0.5 * x * (1.0 + jnp.tanh(math.sqrt(2.0 / math.pi) * (x + 0.044715 * (x * x * x))))


def _dot(a, b):
    return jnp.dot(a, b, preferred_element_type=F32)


def _dot_nt(a, b):
    return lax.dot_general(a, b, NT_DIMS, preferred_element_type=F32)


def _mix_in(x, gmix, win, lng, lnb):
    xn = _rms(x, gmix).astype(BF16)
    h = _dot(xn, win)
    z = _gelu(h[:, :2 * MIX_A])
    u = z[:, :MIX_A]
    vs = []
    for g in range(A_GROUPS):
        seg = z[:, MIX_A + g * A_GW:MIX_A + (g + 1) * A_GW]
        mu = jnp.mean(seg, axis=-1, keepdims=True)
        cen = seg - mu
        var = jnp.mean(cen * cen, axis=-1, keepdims=True)
        vs.append(cen * lax.rsqrt(var + EPS) * lng[:, g * A_GW:(g + 1) * A_GW] + lnb[:, g * A_GW:(g + 1) * A_GW])
    v = jnp.concatenate(vs, axis=1)
    r0 = 2 * MIX_A
    q = h[:, r0:r0 + MIX_B]
    k = h[:, r0 + MIX_B:r0 + MIX_B + N_KV * HEAD_DIM]
    vb = h[:, r0 + MIX_B + N_KV * HEAD_DIM:]
    return u, v, q, k, vb


def _swa_softmax(s, mask, sk):
    s = jnp.where(mask, s, NEG)
    m = jnp.maximum(jnp.max(s, axis=-1, keepdims=True), sk)
    p = jnp.exp(s - m)
    return p / (jnp.sum(p, axis=-1, keepdims=True) + jnp.exp(sk - m))


def _fill_rel_bias(bias_scr, bkt_ref, relb_ref):
    bkt = bkt_ref[...]
    n_q = bkt.shape[0]
    for h in range(N_HEADS):
        acc = jnp.zeros(bkt.shape, F32)
        for b in range(REL_BUCKETS):
            acc = jnp.where(bkt == b, relb_ref[b:b + 1, h:h + 1], acc)
        bias_scr[h * n_q:(h + 1) * n_q, :] = acc


def _mix_prompt_kernel(x_ref, gmix_ref, win_ref, lng_ref, lnb_ref, wm_ref, bs_ref, bkt_ref, relb_ref, sink_ref,
                       ga_ref, gb_ref, woa_ref, wob_ref, h1_ref, k_ref, v_ref,
                       kprev, vprev, ya_scr, ob_scr, bias_ref, *, rows, tiles_per_seq):
    i = pl.program_id(0)

    @pl.when(i == 0)
    def _():
        _fill_rel_bias(bias_ref, bkt_ref, relb_ref)

    x = x_ref[...]
    u, v, q, k, vb = _mix_in(x, gmix_ref[...], win_ref[...], lng_ref[...], lnb_ref[...])
    k_ref[...] = k
    v_ref[...] = vb
    first_tile = (i % tiles_per_seq) == 0

    @pl.when(first_tile)
    def _():
        kprev[...] = jnp.zeros_like(kprev)
        vprev[...] = jnp.zeros_like(vprev)

    nq = N_HEADS * WINDOW
    t_idx = lax.broadcasted_iota(jnp.int32, (nq, 2 * WINDOW), 0) % WINDOW
    kj = lax.broadcasted_iota(jnp.int32, (nq, 2 * WINDOW), 1)
    band = (kj > t_idx) & (kj <= t_idx + WINDOW)
    kj_min = jnp.where(first_tile, WINDOW, 0)
    band_first = band & (kj >= kj_min)
    lane = lax.broadcasted_iota(jnp.int32, (1, LANES), 1)
    lane_c = [lane < HEAD_DIM, lane >= HEAD_DIM]
    sk = sink_ref[...]
    bias = bias_ref[...]

    nblk = rows // WINDOW

    def block_rows(j):
        return slice(j * WINDOW, (j + 1) * WINDOW)

    def scores(j):
        r = block_rows(j)
        k_prev = kprev[...] if j == 0 else k[block_rows(j - 1)]
        kcat = jnp.concatenate([k_prev, k[r]], axis=0).astype(BF16)
        qg = jnp.concatenate([q[r, g * LANES:(g + 1) * LANES] for g in range(GQA)], axis=0)
        qs = jnp.concatenate([jnp.where(lane_c[c], qg, 0.0) for c in range(N_KV)], axis=0).astype(BF16)
        return _dot_nt(qs, kcat) * (1.0 / math.sqrt(HEAD_DIM)) + bias

    s_next = scores(0)
    for j in range(nblk):
        r = block_rows(j)
        s = s_next
        if j + 1 < nblk:
            s_next = scores(j + 1)
        for g in range(A_GROUPS):
            c = slice(g * A_GW, (g + 1) * A_GW)
            gate = _dot(wm_ref[g], v[r, c].astype(BF16)) + bs_ref[g]
            ya_scr[r, c] = u[r, c] * gate
        v_prev = vprev[...] if j == 0 else vb[block_rows(j - 1)]
        vcat = jnp.concatenate([v_prev, vb[r]], axis=0)
        p = _swa_softmax(s, band_first if j == 0 else band, sk).astype(BF16)
        half = GQA * WINDOW
        o = (_dot(p[:half], jnp.where(lane_c[0], vcat, 0.0).astype(BF16))
             + _dot(p[half:], jnp.where(lane_c[1], vcat, 0.0).astype(BF16)))
        for g in range(GQA):
            ob_scr[r, g * LANES:(g + 1) * LANES] = o[g * WINDOW:(g + 1) * WINDOW]

    kprev[...] = k[rows - WINDOW:]
    vprev[...] = vb[rows - WINDOW:]
    ya = _rms(ya_scr[...], ga_ref[...]).astype(BF16)
    ob = _rms(ob_scr[...], gb_ref[...]).astype(BF16)
    h1_ref[...] = x + _dot(ya, woa_ref[...]) + _dot(ob, wob_ref[...])


def _mix_sample_kernel(x_ref, ck_ref, cv_ref, gmix_ref, win_ref, lng_ref, lnb_ref, wm_ref, bs_ref, bkt_ref,
                       relb_ref, sink_ref, ga_ref, gb_ref, woa_ref, wob_ref,
                       h1_ref, ko_ref, vo_ref, cvout_ref, kk_scr, vv_scr, bias_ref, *, nseq, seq_len):
    i = pl.program_id(0)
    wbuf = WINDOW

    @pl.when(i == 0)
    def _():
        kk_scr[...] = jnp.zeros_like(kk_scr)
        vv_scr[...] = jnp.zeros_like(vv_scr)
        _fill_rel_bias(bias_ref, bkt_ref, relb_ref)

    x = x_ref[...]
    u, v, q, k, vb = _mix_in(x, gmix_ref[...], win_ref[...], lng_ref[...], lnb_ref[...])
    cvout_ref[...] = v
    ya_parts = []
    for g in range(A_GROUPS):
        c = slice(g * A_GW, (g + 1) * A_GW)
        s = _dot(wm_ref[g], v[:, c].astype(BF16)) + bs_ref[g]
        ya_parts.append(u[:, c] * s)
    ya = jnp.concatenate(ya_parts, axis=1)

    kk_scr[:, 0:wbuf, :] = ck_ref[...]
    vv_scr[:, 0:wbuf, :] = cv_ref[...]
    kk_scr[:, wbuf:wbuf + seq_len, :] = k.reshape(nseq, seq_len, LANES)
    vv_scr[:, wbuf:wbuf + seq_len, :] = vb.reshape(nseq, seq_len, LANES)
    ko_ref[...] = kk_scr[:, seq_len:wbuf + seq_len, :]
    vo_ref[...] = vv_scr[:, seq_len:wbuf + seq_len, :]

    nq = N_HEADS * seq_len
    l_idx = lax.broadcasted_iota(jnp.int32, (nq, 2 * WINDOW), 0) % seq_len
    kj = lax.broadcasted_iota(jnp.int32, (nq, 2 * WINDOW), 1)
    band = (kj > l_idx) & (kj <= l_idx + wbuf)
    lane = lax.broadcasted_iota(jnp.int32, (1, 1, LANES), 2)
    lane_c = [lane < HEAD_DIM, lane >= HEAD_DIM]

    q3 = q.reshape(nseq, seq_len, MIX_B)
    qg = jnp.concatenate([q3[:, :, g * LANES:(g + 1) * LANES] for g in range(GQA)], axis=1)
    qs = jnp.concatenate([jnp.where(lane_c[c], qg, 0.0) for c in range(N_KV)], axis=1).astype(BF16)
    kk = kk_scr[...].astype(BF16)
    vv = vv_scr[...]
    s = jnp.einsum('nqe,nke->nqk', qs, kk, preferred_element_type=F32) * (1.0 / math.sqrt(HEAD_DIM))
    s = s + bias_ref[...][None]
    p = _swa_softmax(s, band[None], sink_ref[...][None]).astype(BF16)
    half = GQA * seq_len
    o = (jnp.einsum('nqk,nke->nqe', p[:, :half], jnp.where(lane_c[0], vv, 0.0).astype(BF16),
                    preferred_element_type=F32)
         + jnp.einsum('nqk,nke->nqe', p[:, half:], jnp.where(lane_c[1], vv, 0.0).astype(BF16),
                      preferred_element_type=F32))
    ob3 = jnp.concatenate([o[:, g * seq_len:(g + 1) * seq_len, :] for g in range(GQA)], axis=2)
    ob = ob3.reshape(nseq * seq_len, MIX_B)
    ya_n = _rms(ya, ga_ref[...]).astype(BF16)
    ob_n = _rms(ob, gb_ref[...]).astype(BF16)
    h1_ref[...] = x + _dot(ya_n, woa_ref[...]) + _dot(ob_n, wob_ref[...])


def _mem_kv_kernel(mem_ref, g_ref, wk_ref, wv_ref, mk_ref, mv_ref):
    mn = _rms(mem_ref[0], g_ref[...]).astype(BF16)
    mk_ref[0] = _dot(mn, wk_ref[...])
    mv_ref[0] = _dot(mn, wv_ref[...])


def _mem_tail(h1, o, wmo_ref, gffn_ref, wq_ref, h2_ref, xn_ref, qp_ref):
    h2 = h1 + _dot(o.astype(BF16), wmo_ref[...])
    h2_ref[...] = h2
    xn = _rms(h2, gffn_ref[...]).astype(BF16)
    xn_ref[...] = xn
    qp = _dot(xn, wq_ref[...]).astype(BF16)
    for hp in range(2 * PEER_HEADS):
        qp_ref[hp] = qp[:, hp * PEER_HALF:(hp + 1) * PEER_HALF]


def _mem_prompt_kernel(h1_ref, mk_ref, mv_ref, gmem_ref, wmq_ref, wmo_ref, gffn_ref, wq_ref,
                       h2_ref, xn_ref, qp_ref):
    h1 = h1_ref[...]
    q = _dot(_rms(h1, gmem_ref[...]).astype(BF16), wmq_ref[...]).astype(BF16)
    mk = mk_ref[0].astype(BF16)
    mv = mv_ref[0].astype(BF16)
    def head_cols(hh):
        return slice(hh * MEM_HEAD_DIM, (hh + 1) * MEM_HEAD_DIM)

    def scores(hh):
        c = head_cols(hh)
        return _dot_nt(q[:, c], mk[:, c]) * (1.0 / math.sqrt(MEM_HEAD_DIM))

    outs = []
    s_next = scores(0)
    for hh in range(MEM_HEADS):
        c = head_cols(hh)
        s = s_next
        if hh + 1 < MEM_HEADS:
            s_next = scores(hh + 1)
        m = jnp.max(s, axis=-1, keepdims=True)
        p = jnp.exp(s - m)
        p = (p / jnp.sum(p, axis=-1, keepdims=True)).astype(BF16)
        outs.append(_dot(p, mv[:, c]))
    o = jnp.concatenate(outs, axis=1)
    _mem_tail(h1, o, wmo_ref, gffn_ref, wq_ref, h2_ref, xn_ref, qp_ref)


def _mem_sample_kernel(h1_ref, ck_ref, cv_ref, gmem_ref, wmq_ref, wmo_ref, gffn_ref, wq_ref,
                       h2_ref, xn_ref, qp_ref, *, nseq, seq_len):
    h1 = h1_ref[...]
    q = _dot(_rms(h1, gmem_ref[...]).astype(BF16), wmq_ref[...])
    q3 = q.reshape(nseq, seq_len, MEM_INNER)
    qs = jnp.concatenate([q3[:, :, hh * MEM_HEAD_DIM:(hh + 1) * MEM_HEAD_DIM] for hh in range(MEM_HEADS)],
                         axis=1).astype(BF16)
    ck = ck_ref[...].astype(BF16)
    cv = cv_ref[...].astype(BF16)
    s = jnp.einsum('nqd,nkd->nqk', qs, ck, preferred_element_type=F32) * (1.0 / math.sqrt(MEM_HEAD_DIM))
    nq, nk = MEM_HEADS * seq_len, N_MEM * MEM_HEADS
    row_head = lax.broadcasted_iota(jnp.int32, (nq, nk), 0) // seq_len
    col_head = lax.broadcasted_iota(jnp.int32, (nq, nk), 1) % MEM_HEADS
    s = jnp.where((row_head == col_head)[None], s, -jnp.inf)
    m = jnp.max(s, axis=-1, keepdims=True)
    p = jnp.exp(s - m)
    p = (p / jnp.sum(p, axis=-1, keepdims=True)).astype(BF16)
    of = jnp.einsum('nqk,nkd->nqd', p, cv, preferred_element_type=F32)
    o3 = jnp.concatenate([of[:, hh * seq_len:(hh + 1) * seq_len] for hh in range(MEM_HEADS)], axis=2)
    o = o3.reshape(nseq * seq_len, MEM_INNER)
    _mem_tail(h1, o, wmo_ref, gffn_ref, wq_ref, h2_ref, xn_ref, qp_ref)


def _sort_network(n):
    out, p = [], 1
    while p < n:
        k = p
        while k >= 1:
            for j in range(k % p, n - k, 2 * k):
                for i in range(min(k, n - j - k)):
                    if (i + j) // (2 * p) == (i + j + k) // (2 * p):
                        out.append((i + j, i + j + k))
            k //= 2
        p *= 2
    return out


_SORT16 = _sort_network(PEER_TOPK)


def _compare_exchange(x, i, j):
    hi, lo = jnp.maximum(x[i], x[j]), jnp.minimum(x[i], x[j])
    x[i], x[j] = hi, lo


def _top16_sorted(slabs):
    n = len(slabs)
    x = list(slabs)
    for i, j in _SORT16:
        if j < n:
            _compare_exchange(x, i, j)
    for shift in (4, 2, 1):
        y = []
        for i in range(PEER_TOPK):
            lo = x[i] if i < n else None
            k = PEER_TOPK - 1 - i
            hi = pltpu.roll(x[k], shift, axis=0) if k < n else None
            y.append(hi if lo is None else lo if hi is None else jnp.maximum(lo, hi))
        d = PEER_TOPK // 2
        while d >= 1:
            for i in range(PEER_TOPK):
                if i & d == 0:
                    _compare_exchange(y, i, i + d)
            d //= 2
        x, n = y, PEER_TOPK
    return x


def _step_count(slabs, thresholds, below):
    th = thresholds
    assert len(th) == PEER_TOPK == 16
    hit = (lambda x, t: x < t) if below else (lambda x, t: x >= t)
    out = []
    for x in slabs:
        c8 = hit(x, th[7])
        c4 = hit(x, jnp.where(c8, th[11], th[3]))
        c2 = hit(x, jnp.where(c8, jnp.where(c4, th[13], th[9]), jnp.where(c4, th[5], th[1])))
        upper = jnp.where(c4, jnp.where(c2, th[14], th[12]), jnp.where(c2, th[10], th[8]))
        lower = jnp.where(c4, jnp.where(c2, th[6], th[4]), jnp.where(c2, th[2], th[0]))
        c1 = hit(x, jnp.where(c8, upper, lower))
        count = (jnp.where(c8, 8.0, 0.0) + jnp.where(c4, 4.0, 0.0)) + (jnp.where(c2, 2.0, 0.0) + jnp.where(c1, 1.0, 0.0))
        out.append(jnp.where(hit(x, th[15]), 16.0, count))
    return out


def _peer_prep_kernel(qp_ref, keys_ref, r2_ref, e2_ref, cnt_ref, e1_ref, *, lane_tiles):
    nslab = PEER_KEYS // SUBLANES

    def per_head(h, carry):
        for lt in range(lane_tiles):
            lanes = slice(lt * LANES, (lt + 1) * LANES)
            q1 = qp_ref[2 * h, lanes, :]
            q2 = qp_ref[2 * h + 1, lanes, :]
            s1 = _dot_nt(keys_ref[h, 0], q1)
            s2 = _dot_nt(keys_ref[h, 1], q2)
            s1s = [s1[r * SUBLANES:(r + 1) * SUBLANES] for r in range(nslab)]
            s2s = [s2[r * SUBLANES:(r + 1) * SUBLANES] for r in range(nslab)]
            a = _top16_sorted(s1s)
            b = _top16_sorted(s2s)
            sub = lax.broadcasted_iota(jnp.int32, a[0].shape, 0)

            def by_sublane(vals):
                out = vals[SUBLANES - 1]
                for r in range(SUBLANES - 2, -1, -1):
                    out = jnp.where(sub == r, vals[r], out)
                return out

            a_lo, a_hi = by_sublane(a[:8]), by_sublane(a[8:])
            b_lo, b_hi = by_sublane(b[:8]), by_sublane(b[8:])
            tail = lambda x: jnp.where(sub >= 3, x, -jnp.inf)
            cand = [a_lo + b[0], a_hi + b[0], a_lo + b[1], a_lo + b[2], tail(a[0] + b_lo), a[0] + b_hi,
                    tail(a[1] + b_lo), tail(a[2] + b_lo), tail(a[3] + b_lo)]
            tops = _top16_sorted(cand)
            tau = tops[-1]
            z = jnp.ones_like(tau)
            for t in tops[1:]:
                z = z + jnp.exp(t - tops[0])
            thr = []
            for j in range(PEER_TOPK):
                t = jnp.full_like(tau, jnp.inf)
                for i in range(PEER_TOPK // (j + 1)):
                    t = jnp.where(a[i] + b[j] >= tau, a[i], t)
                thr.append(t)
            cnt = _step_count(s1s, thr, below=False)
            rank2 = _step_count(s2s, b, below=True)
            half_inv_z = 0.5 / z
            r2_ref[h, :, lanes] = jnp.concatenate(rank2, axis=0).astype(BF16)
            e2_ref[h, :, lanes] = jnp.concatenate([jnp.exp(x - b[0]) * half_inv_z for x in s2s], axis=0).astype(BF16)
            cnt_ref[h, :, lanes] = jnp.concatenate(cnt, axis=0)
            e1_ref[h, :, lanes] = jnp.concatenate([jnp.exp(x - a[0]) for x in s1s], axis=0)
        return carry

    lax.fori_loop(0, PEER_HEADS, per_head, 0)


def _peer_weight_rows(j, row, lanes, cnt_slab, e1_slab, x_ref, p_ref, r2_ref, e2_ref):
    sub, width = SUBLANES, lanes.stop - lanes.start
    rows = slice(j * PEER_KEYS, (j + 1) * PEER_KEYS)
    x = x_ref[rows, lanes]
    c0 = math.sqrt(2.0 / math.pi)
    act = (x + x * jnp.tanh(x * ((0.044715 * c0) * (x * x) + c0))).astype(BF16)
    w = None
    for h in range(PEER_HEADS):
        cnt = jnp.broadcast_to(cnt_slab[h][row:row + 1, lanes], (sub, width)).astype(BF16)
        e1 = jnp.broadcast_to(e1_slab[h][row:row + 1, lanes], (sub, width)).astype(BF16)
        cnt = jnp.concatenate([cnt] * (PEER_KEYS // sub), axis=0)
        e1 = jnp.concatenate([e1] * (PEER_KEYS // sub), axis=0)
        e2 = e2_ref[h, :, lanes]
        term = jnp.where(r2_ref[h, :, lanes] < cnt, e2, jnp.zeros_like(e2)) * e1
        w = term if w is None else w + term
    p_ref[rows, lanes] = w * act


def _interleave(mxu_chunks, vpu_chunks):
    per = -(-len(vpu_chunks) // max(len(mxu_chunks), 1))
    for c, mxu in enumerate(mxu_chunks):
        mxu()
        for vpu in vpu_chunks[c * per:(c + 1) * per]:
            vpu()
    for vpu in vpu_chunks[len(mxu_chunks) * per:]:
        vpu()


def _peer_ffn_kernel(xn_ref, h2_ref, eu_ref, evt_ref, r2_ref, e2_ref, cnt_ref, e1_ref, gfin_ref,
                     y_ref, acc_ref, x_a, x_b, p_a, p_b, *, tokens, i1_per_step, n_parts):
    e = pl.program_id(1)
    part_keys = i1_per_step // n_parts
    part = part_keys * PEER_KEYS
    width = 2 * LANES
    n_lt = tokens // width
    d_rows = D_MODEL // 2
    xs, ps = (x_a, x_b), (p_a, p_b)

    @pl.when(e == 0)
    def _():
        acc_ref[...] = jnp.zeros_like(acc_ref)

    i1_0 = pl.multiple_of(e * i1_per_step, i1_per_step)
    cnt_slab = [cnt_ref[h, pl.ds(i1_0, i1_per_step), :] for h in range(PEER_HEADS)]
    e1_slab = [e1_ref[h, pl.ds(i1_0, i1_per_step), :] for h in range(PEER_HEADS)]

    def u_product(t, lt):
        lanes = slice(lt * width, (lt + 1) * width)
        xs[t % 2][:, lanes] = _dot_nt(eu_ref[t * part:(t + 1) * part, :], xn_ref[lanes, :])

    def v_product(t, dh, lt):
        rows = slice(dh * d_rows, (dh + 1) * d_rows)
        lanes = slice(lt * width, (lt + 1) * width)
        acc_ref[rows, lanes] += _dot(evt_ref[rows, t * part:(t + 1) * part], ps[t % 2][:, lanes])

    def weights(t):
        return [functools.partial(_peer_weight_rows, j, t * part_keys + j, slice(lt * width, (lt + 1) * width),
                                  cnt_slab, e1_slab, xs[t % 2], ps[t % 2], r2_ref, e2_ref)
                for lt in range(n_lt) for j in range(part_keys)]

    for t in range(n_parts + 2):
        mxu = []
        if t < n_parts:
            mxu += [functools.partial(u_product, t, lt) for lt in range(n_lt)]
        if t >= 2:
            mxu += [functools.partial(v_product, t - 2, dh, lt) for lt in range(n_lt) for dh in range(2)]
        _interleave(mxu, weights(t - 1) if 1 <= t <= n_parts else [])

    @pl.when(e == pl.num_programs(1) - 1)
    def _():
        y_ref[...] = _rms(h2_ref[...] + acc_ref[...].T, gfin_ref[...])


def _full(shape):
    return pl.BlockSpec(shape, lambda *_: (0,) * len(shape))


def _t5_bucket(d):
    n = np.maximum(d, 0)
    nf = np.maximum(n, 1).astype(np.float32)
    scaled = (np.log(nf / np.float32(REL_MAX_EXACT)) / np.float32(math.log(REL_MAX_DIST / REL_MAX_EXACT))
              * np.float32(REL_BUCKETS - REL_MAX_EXACT))
    frac = np.abs(scaled - np.round(scaled))
    capped = scaled > REL_BUCKETS - REL_MAX_EXACT
    assert np.all((frac > 1e-3) | (frac < 1e-5) | capped), "bucket boundary too close to an integer distance"
    large = np.minimum(REL_MAX_EXACT + np.floor(scaled + 1e-4).astype(np.int32), REL_BUCKETS - 1)
    return np.where(n < REL_MAX_EXACT, n, large).astype(np.int32)


def _bucket_table(n_q):
    d = (np.arange(n_q)[:, None] + WINDOW) - np.arange(2 * WINDOW)[None, :]
    return jnp.asarray(_t5_bucket(d))


def _head_perm():
    g, c, d = jnp.meshgrid(jnp.arange(GQA), jnp.arange(N_KV), jnp.arange(HEAD_DIM), indexing='ij')
    return ((c * GQA + g) * HEAD_DIM + d).reshape(-1)


def _mix_weights(norm_mix_g, w_in, ln_v_g, ln_v_b, norm_a_out_g, norm_b_out_g, w_out):
    perm = _head_perm()
    q0 = 2 * MIX_A
    win = jnp.concatenate([w_in[:, :q0], w_in[:, q0:q0 + MIX_B][:, perm], w_in[:, q0 + MIX_B:]], axis=1).astype(BF16)
    woa = w_out[:MIX_A].astype(BF16)
    wob = w_out[MIX_A:][perm].astype(BF16)
    gb = norm_b_out_g[perm][None]
    return (norm_mix_g[None], win, ln_v_g[None], ln_v_b[None], norm_a_out_g[None], gb, woa, wob)


def _mix_prompt(x, wts, wm, bs, bkt, relb, sink, *, rows, seq):
    gmix, win, lng, lnb, ga, gb, woa, wob = wts
    t = x.shape[0]
    row_spec = lambda w: pl.BlockSpec((rows, w), lambda i: (i, 0))
    kern = functools.partial(_mix_prompt_kernel, rows=rows, tiles_per_seq=seq // rows)
    return pl.pallas_call(
        kern,
        grid=(t // rows,),
        in_specs=[row_spec(D_MODEL), _full(gmix.shape), _full(win.shape), _full(lng.shape), _full(lnb.shape),
                  _full(wm.shape), _full(bs.shape), _full(bkt.shape), _full(relb.shape), _full(sink.shape),
                  _full(ga.shape), _full(gb.shape), _full(woa.shape), _full(wob.shape)],
        out_specs=[row_spec(D_MODEL), row_spec(LANES), row_spec(LANES)],
        out_shape=[jax.ShapeDtypeStruct((t, D_MODEL), F32), jax.ShapeDtypeStruct((t, LANES), F32),
                   jax.ShapeDtypeStruct((t, LANES), F32)],
        scratch_shapes=[pltpu.VMEM((WINDOW, LANES), F32), pltpu.VMEM((WINDOW, LANES), F32),
                        pltpu.VMEM((rows, MIX_A), F32), pltpu.VMEM((rows, MIX_B), F32),
                        pltpu.VMEM((N_HEADS * WINDOW, 2 * WINDOW), F32)],
        compiler_params=_cparams(1),
        name="mix_prompt",
    )(x, gmix, win, lng, lnb, wm, bs, bkt, relb, sink, ga, gb, woa, wob)


def _mix_sample(x, ck, cv, wts, wm, bs, bkt, relb, sink, *, nseq, seq_len):
    gmix, win, lng, lnb, ga, gb, woa, wob = wts
    t = x.shape[0]
    rows = nseq * seq_len
    row_spec = lambda w: pl.BlockSpec((rows, w), lambda i: (i, 0))
    seq_spec = pl.BlockSpec((nseq, WINDOW, LANES), lambda i: (i, 0, 0))
    kern = functools.partial(_mix_sample_kernel, nseq=nseq, seq_len=seq_len)
    return pl.pallas_call(
        kern,
        grid=(t // rows,),
        in_specs=[row_spec(D_MODEL), seq_spec, seq_spec, _full(gmix.shape), _full(win.shape), _full(lng.shape),
                  _full(lnb.shape), _full(wm.shape), _full(bs.shape), _full(bkt.shape), _full(relb.shape),
                  _full(sink.shape),
                  _full(ga.shape), _full(gb.shape), _full(woa.shape), _full(wob.shape)],
        out_specs=[row_spec(D_MODEL), seq_spec, seq_spec, row_spec(MIX_A)],
        out_shape=[jax.ShapeDtypeStruct((t, D_MODEL), F32),
                   jax.ShapeDtypeStruct(ck.shape, F32), jax.ShapeDtypeStruct(cv.shape, F32),
                   jax.ShapeDtypeStruct((t, MIX_A), F32)],
        scratch_shapes=[pltpu.VMEM((nseq, 2 * WINDOW, LANES), F32), pltpu.VMEM((nseq, 2 * WINDOW, LANES), F32),
                        pltpu.VMEM((N_HEADS * seq_len, 2 * WINDOW), F32)],
        compiler_params=_cparams(1),
        name="mix_sample",
    )(x, ck, cv, gmix, win, lng, lnb, wm, bs, bkt, relb, sink, ga, gb, woa, wob)


def _mem_kv(mem, g, wk, wv):
    b, m, d = mem.shape
    return pl.pallas_call(
        _mem_kv_kernel,
        grid=(b,),
        in_specs=[pl.BlockSpec((1, m, d), lambda i: (i, 0, 0)), _full(g.shape), _full(wk.shape), _full(wv.shape)],
        out_specs=[pl.BlockSpec((1, m, MEM_INNER), lambda i: (i, 0, 0))] * 2,
        out_shape=[jax.ShapeDtypeStruct((b, m, MEM_INNER), F32)] * 2,
        compiler_params=_cparams(1),
        name="mem_kv",
    )(mem, g, wk, wv)


def _mem_outs(t, rows):
    row_spec = lambda w: pl.BlockSpec((rows, w), lambda i: (i, 0))
    specs = [row_spec(D_MODEL), row_spec(D_MODEL),
             pl.BlockSpec((2 * PEER_HEADS, rows, PEER_HALF), lambda i: (0, i, 0))]
    shapes = [jax.ShapeDtypeStruct((t, D_MODEL), F32), jax.ShapeDtypeStruct((t, D_MODEL), BF16),
              jax.ShapeDtypeStruct((2 * PEER_HEADS, t, PEER_HALF), BF16)]
    return specs, shapes


def _mem_prompt(h1, mk, mv, mwts, *, rows, seq):
    t = h1.shape[0]
    tiles_per_seq = seq // rows
    out_specs, out_shapes = _mem_outs(t, rows)
    kv_spec = pl.BlockSpec((1, N_MEM, MEM_INNER), lambda i: (i // tiles_per_seq, 0, 0))
    return pl.pallas_call(
        _mem_prompt_kernel,
        grid=(t // rows,),
        in_specs=[pl.BlockSpec((rows, D_MODEL), lambda i: (i, 0)), kv_spec, kv_spec] + [_full(w.shape) for w in mwts],
        out_specs=out_specs,
        out_shape=out_shapes,
        compiler_params=_cparams(1),
        name="mem_prompt",
    )(h1, mk, mv, *mwts)


def _mem_sample(h1, ck, cv, mwts, *, nseq, seq_len):
    t = h1.shape[0]
    rows = nseq * seq_len
    out_specs, out_shapes = _mem_outs(t, rows)
    kv_spec = pl.BlockSpec((nseq, N_MEM * MEM_HEADS, MEM_HEAD_DIM), lambda i: (i, 0, 0))
    kern = functools.partial(_mem_sample_kernel, nseq=nseq, seq_len=seq_len)
    return pl.pallas_call(
        kern,
        grid=(t // rows,),
        in_specs=[pl.BlockSpec((rows, D_MODEL), lambda i: (i, 0)), kv_spec, kv_spec] + [_full(w.shape) for w in mwts],
        out_specs=out_specs,
        out_shape=out_shapes,
        compiler_params=_cparams(1),
        name="mem_sample",
    )(h1, ck, cv, *mwts)


def _peer_prep(qp, keys, *, tokens):
    t = qp.shape[1]
    out_spec = pl.BlockSpec((PEER_HEADS, PEER_KEYS, tokens), lambda i: (0, 0, i))
    kern = functools.partial(_peer_prep_kernel, lane_tiles=tokens // LANES)
    return pl.pallas_call(
        kern,
        grid=(t // tokens,),
        in_specs=[pl.BlockSpec((2 * PEER_HEADS, tokens, PEER_HALF), lambda i: (0, i, 0)), _full(keys.shape)],
        out_specs=[out_spec] * 4,
        out_shape=[jax.ShapeDtypeStruct((PEER_HEADS, PEER_KEYS, t), dt) for dt in (BF16, BF16, F32, F32)],
        compiler_params=_cparams(1),
        name="peer_prep",
    )(qp, keys)


def _peer_ffn(xn, h2, eu, evt, prep, gfin, *, tokens, i1_per_step):
    t = xn.shape[0]
    experts = i1_per_step * PEER_KEYS
    tok_spec = pl.BlockSpec((tokens, D_MODEL), lambda i, e: (i, 0))
    prep_spec = pl.BlockSpec((PEER_HEADS, PEER_KEYS, tokens), lambda i, e: (0, 0, i))
    n_parts = 4
    part = experts // n_parts
    kern = functools.partial(_peer_ffn_kernel, tokens=tokens, i1_per_step=i1_per_step, n_parts=n_parts)
    return pl.pallas_call(
        kern,
        grid=(t // tokens, PEER_N // experts),
        in_specs=[tok_spec, tok_spec,
                  pl.BlockSpec((experts, D_MODEL), lambda i, e: (e, 0)),
                  pl.BlockSpec((D_MODEL, experts), lambda i, e: (0, e)),
                  prep_spec, prep_spec, prep_spec, prep_spec,
                  pl.BlockSpec(gfin.shape, lambda i, e: (0, 0))],
        out_specs=tok_spec,
        out_shape=jax.ShapeDtypeStruct((t, D_MODEL), F32),
        scratch_shapes=[pltpu.VMEM((D_MODEL, tokens), F32),
                        pltpu.VMEM((part, tokens), F32), pltpu.VMEM((part, tokens), F32),
                        pltpu.VMEM((part, tokens), BF16), pltpu.VMEM((part, tokens), BF16)],
        compiler_params=_cparams(2),
        name="peer_ffn",
    )(xn, h2, eu, evt, *prep, gfin)


def kernel(x_prompt, x_sample, mem_prompt, cache_swa_k, cache_swa_v, cache_mem_k, cache_mem_v, norm_mix_g, w_in, ln_v_g, ln_v_b, spatial_w, spatial_b, attn_sinks, rel_bias, norm_a_out_g, norm_b_out_g, w_out, norm_mem_g, norm_memsrc_g, w_mq, w_mk, w_mv, w_mo, norm_ffn_g, peer_wq, peer_keys, peer_u, peer_v, norm_final_g):
    batch, seq, _ = x_prompt.shape
    nsamp, dec_len, _ = x_sample.shape
    l = 0

    wts = _mix_weights(norm_mix_g[l], w_in[l], ln_v_g[l], ln_v_b[l],
                       norm_a_out_g[l], norm_b_out_g[l], w_out[l])
    tril = jnp.tril(jnp.ones((CHUNK, CHUNK), F32))
    wm = spatial_w[l] * tril
    wm_p = wm.astype(BF16)
    bs_p = spatial_b[l][:, :, None]
    seqs_per_chunk = CHUNK // dec_len
    eye = jnp.eye(seqs_per_chunk, dtype=F32)
    wm_s = jnp.einsum('ab,gts->gatbs', eye, wm[:, :dec_len, :dec_len]).reshape(A_GROUPS, CHUNK, CHUNK).astype(BF16)
    bs_s = jnp.tile(spatial_b[l][:, :dec_len], (1, seqs_per_chunk))[:, :, None]
    bkt_p = _bucket_table(WINDOW)
    bkt_s = _bucket_table(dec_len)
    sink_p = jnp.repeat(attn_sinks[l], WINDOW)[:, None]
    sink_s = jnp.repeat(attn_sinks[l], dec_len)[:, None]
    mwts = (norm_mem_g[l][None], w_mq[l].astype(BF16), w_mo[l].astype(BF16), norm_ffn_g[l][None],
            peer_wq[l].astype(BF16))
    keys = peer_keys[l].astype(BF16)
    eu = peer_u[l].astype(BF16)
    evt = peer_v[l].T.astype(BF16)
    gfin = norm_final_g[None]

    xp = x_prompt.reshape(batch * seq, D_MODEL)
    h1p, kp, vp = _mix_prompt(xp, wts, wm_p, bs_p, bkt_p, rel_bias, sink_p, rows=512, seq=seq)
    mk, mv = _mem_kv(mem_prompt, norm_memsrc_g[l][None], w_mk[l].astype(BF16), w_mv[l].astype(BF16))
    h2p, xnp_, qpp = _mem_prompt(h1p, mk, mv, mwts, rows=512, seq=seq)
    prep_p = _peer_prep(qpp, keys, tokens=1024)
    yp = _peer_ffn(xnp_, h2p, eu, evt, prep_p, gfin, tokens=512, i1_per_step=16)

    xs = x_sample.reshape(nsamp * dec_len, D_MODEL)
    ck = cache_swa_k.reshape(nsamp, WINDOW, LANES)
    cv = cache_swa_v.reshape(nsamp, WINDOW, LANES)
    h1s, ks, vs, chunk_v = _mix_sample(xs, ck, cv, wts, wm_s, bs_s, bkt_s, rel_bias, sink_s,
                                       nseq=seqs_per_chunk, seq_len=dec_len)
    cmk = cache_mem_k.reshape(nsamp, N_MEM * MEM_HEADS, MEM_HEAD_DIM)
    cmv = cache_mem_v.reshape(nsamp, N_MEM * MEM_HEADS, MEM_HEAD_DIM)
    h2s, xns, qps = _mem_sample(h1s, cmk, cmv, mwts, nseq=8, seq_len=dec_len)
    prep_s = _peer_prep(qps, keys, tokens=1024)
    ys = _peer_ffn(xns, h2s, eu, evt, prep_s, gfin, tokens=512, i1_per_step=16)

    kp4 = kp.reshape(batch, seq, LANES)[:, -WINDOW:].reshape(batch, WINDOW, N_KV, HEAD_DIM)
    vp4 = vp.reshape(batch, seq, LANES)[:, -WINDOW:].reshape(batch, WINDOW, N_KV, HEAD_DIM)
    return (yp.reshape(batch, seq, D_MODEL),
            ys.reshape(nsamp, dec_len, D_MODEL),
            kp4[None], vp4[None],
            mk.reshape(batch, N_MEM, MEM_HEADS, MEM_HEAD_DIM)[None],
            mv.reshape(batch, N_MEM, MEM_HEADS, MEM_HEAD_DIM)[None],
            ks.reshape(nsamp, WINDOW, N_KV, HEAD_DIM)[None],
            vs.reshape(nsamp, WINDOW, N_KV, HEAD_DIM)[None],
            chunk_v.reshape(nsamp, dec_len, A_GROUPS, A_GW)[None])
```

```python
import functools
import math

import jax
import jax.numpy as jnp
import numpy as np
from jax import lax
from jax.experimental import pallas as pl
from jax.experimental.pallas import tpu as pltpu

F32 = jnp.float32
BF16 = jnp.bfloat16

D_MODEL = 1024
MIX_A = 512
A_GROUPS = 4
A_GW = 128
CHUNK = 128
N_HEADS = 8
N_KV = 2
HEAD_DIM = 64
GQA = 4
MIX_B = 512
WINDOW = 128
IN_COLS = 1792
REL_BUCKETS = 32
REL_MAX_EXACT = 16
REL_MAX_DIST = 128
N_MEM = 256
MEM_HEADS = 4
MEM_HEAD_DIM = 128
MEM_INNER = 512
PEER_HEADS = 8
PEER_KEYS = 128
PEER_N = PEER_KEYS * PEER_KEYS
PEER_HALF = 128
PEER_TOPK = 16
EPS = 1e-6
NEG = -1e30

LANES = 128
SUBLANES = 8
VMEM_LIMIT = 56 * 1024 * 1024

NT_DIMS = (((1,), (1,)), ((), ()))


def _cparams(n_axes):
    return pltpu.CompilerParams(dimension_semantics=("arbitrary",) * n_axes, vmem_limit_bytes=VMEM_LIMIT)


def _rms(x, g):
    return x * lax.rsqrt(jnp.mean(x * x, axis=-1, keepdims=True) + EPS) * g


def _gelu(x):
    return 0.5 * x * (1.0 + jnp.tanh(math.sqrt(2.0 / math.pi) * (x + 0.044715 * (x * x * x))))


def _dot(a, b):
    return jnp.dot(a, b, preferred_element_type=F32)


def _dot_nt(a, b):
    return lax.dot_general(a, b, NT_DIMS, preferred_element_type=F32)


def _mix_in(x, gmix, win, lng, lnb):
    xn = _rms(x, gmix).astype(BF16)
    h = _dot(xn, win)
    z = _gelu(h[:, :2 * MIX_A])
    u = z[:, :MIX_A]
    vs = []
    for g in range(A_GROUPS):
        seg = z[:, MIX_A + g * A_GW:MIX_A + (g + 1) * A_GW]
        mu = jnp.mean(seg, axis=-1, keepdims=True)
        cen = seg - mu
        var = jnp.mean(cen * cen, axis=-1, keepdims=True)
        vs.append(cen * lax.rsqrt(var + EPS) * lng[:, g * A_GW:(g + 1) * A_GW] + lnb[:, g * A_GW:(g + 1) * A_GW])
    v = jnp.concatenate(vs, axis=1)
    r0 = 2 * MIX_A
    q = h[:, r0:r0 + MIX_B]
    k = h[:, r0 + MIX_B:r0 + MIX_B + N_KV * HEAD_DIM]
    vb = h[:, r0 + MIX_B + N_KV * HEAD_DIM:]
    return u, v, q, k, vb


def _swa_softmax(s, mask, sk):
    s = jnp.where(mask, s, NEG)
    m = jnp.maximum(jnp.max(s, axis=-1, keepdims=True), sk)
    p = jnp.exp(s - m)
    return p / (jnp.sum(p, axis=-1, keepdims=True) + jnp.exp(sk - m))


def _fill_rel_bias(bias_scr, bkt_ref, relb_ref):
    bkt = bkt_ref[...]
    n_q = bkt.shape[0]
    for h in range(N_HEADS):
        acc = jnp.zeros(bkt.shape, F32)
        for b in range(REL_BUCKETS):
            acc = jnp.where(bkt == b, relb_ref[b:b + 1, h:h + 1], acc)
        bias_scr[h * n_q:(h + 1) * n_q, :] = acc


def _mix_prompt_kernel(x_ref, gmix_ref, win_ref, lng_ref, lnb_ref, wm_ref, bs_ref, bkt_ref, relb_ref, sink_ref,
                       ga_ref, gb_ref, woa_ref, wob_ref, h1_ref, k_ref, v_ref,
                       kprev, vprev, ya_scr, ob_scr, bias_ref, *, rows, tiles_per_seq):
    i = pl.program_id(0)

    @pl.when(i == 0)
    def _():
        _fill_rel_bias(bias_ref, bkt_ref, relb_ref)

    x = x_ref[...]
    u, v, q, k, vb = _mix_in(x, gmix_ref[...], win_ref[...], lng_ref[...], lnb_ref[...])
    k_ref[...] = k
    v_ref[...] = vb
    first_tile = (i % tiles_per_seq) == 0

    @pl.when(first_tile)
    def _():
        kprev[...] = jnp.zeros_like(kprev)
        vprev[...] = jnp.zeros_like(vprev)

    nq = N_HEADS * WINDOW
    t_idx = lax.broadcasted_iota(jnp.int32, (nq, 2 * WINDOW), 0) % WINDOW
    kj = lax.broadcasted_iota(jnp.int32, (nq, 2 * WINDOW), 1)
    band = (kj > t_idx) & (kj <= t_idx + WINDOW)
    kj_min = jnp.where(first_tile, WINDOW, 0)
    band_first = band & (kj >= kj_min)
    lane = lax.broadcasted_iota(jnp.int32, (1, LANES), 1)
    lane_c = [lane < HEAD_DIM, lane >= HEAD_DIM]
    sk = sink_ref[...]
    bias = bias_ref[...]

    nblk = rows // WINDOW

    def block_rows(j):
        return slice(j * WINDOW, (j + 1) * WINDOW)

    def scores(j):
        r = block_rows(j)
        k_prev = kprev[...] if j == 0 else k[block_rows(j - 1)]
        kcat = jnp.concatenate([k_prev, k[r]], axis=0).astype(BF16)
        qg = jnp.concatenate([q[r, g * LANES:(g + 1) * LANES] for g in range(GQA)], axis=0)
        qs = jnp.concatenate([jnp.where(lane_c[c], qg, 0.0) for c in range(N_KV)], axis=0).astype(BF16)
        return _dot_nt(qs, kcat) * (1.0 / math.sqrt(HEAD_DIM)) + bias

    s_next = scores(0)
    for j in range(nblk):
        r = block_rows(j)
        s = s_next
        if j + 1 < nblk:
            s_next = scores(j + 1)
        for g in range(A_GROUPS):
            c = slice(g * A_GW, (g + 1) * A_GW)
            gate = _dot(wm_ref[g], v[r, c].astype(BF16)) + bs_ref[g]
            ya_scr[r, c] = u[r, c] * gate
        v_prev = vprev[...] if j == 0 else vb[block_rows(j - 1)]
        vcat = jnp.concatenate([v_prev, vb[r]], axis=0)
        p = _swa_softmax(s, band_first if j == 0 else band, sk).astype(BF16)
        half = GQA * WINDOW
        o = (_dot(p[:half], jnp.where(lane_c[0], vcat, 0.0).astype(BF16))
             + _dot(p[half:], jnp.where(lane_c[1], vcat, 0.0).astype(BF16)))
        for g in range(GQA):
            ob_scr[r, g * LANES:(g + 1) * LANES] = o[g * WINDOW:(g + 1) * WINDOW]

    kprev[...] = k[rows - WINDOW:]
    vprev[...] = vb[rows - WINDOW:]
    ya = _rms(ya_scr[...], ga_ref[...]).astype(BF16)
    ob = _rms(ob_scr[...], gb_ref[...]).astype(BF16)
    h1_ref[...] = x + _dot(ya, woa_ref[...]) + _dot(ob, wob_ref[...])


def _mix_sample_kernel(x_ref, ck_ref, cv_ref, gmix_ref, win_ref, lng_ref, lnb_ref, wm_ref, bs_ref, bkt_ref,
                       relb_ref, sink_ref, ga_ref, gb_ref, woa_ref, wob_ref,
                       h1_ref, ko_ref, vo_ref, cvout_ref, kk_scr, vv_scr, bias_ref, *, nseq, seq_len):
    i = pl.program_id(0)
    wbuf = WINDOW

    @pl.when(i == 0)
    def _():
        kk_scr[...] = jnp.zeros_like(kk_scr)
        vv_scr[...] = jnp.zeros_like(vv_scr)
        _fill_rel_bias(bias_ref, bkt_ref, relb_ref)

    x = x_ref[...]
    u, v, q, k, vb = _mix_in(x, gmix_ref[...], win_ref[...], lng_ref[...], lnb_ref[...])
    cvout_ref[...] = v
    ya_parts = []
    for g in range(A_GROUPS):
        c = slice(g * A_GW, (g + 1) * A_GW)
        s = _dot(wm_ref[g], v[:, c].astype(BF16)) + bs_ref[g]
        ya_parts.append(u[:, c] * s)
    ya = jnp.concatenate(ya_parts, axis=1)

    kk_scr[:, 0:wbuf, :] = ck_ref[...]
    vv_scr[:, 0:wbuf, :] = cv_ref[...]
    kk_scr[:, wbuf:wbuf + seq_len, :] = k.reshape(nseq, seq_len, LANES)
    vv_scr[:, wbuf:wbuf + seq_len, :] = vb.reshape(nseq, seq_len, LANES)
    ko_ref[...] = kk_scr[:, seq_len:wbuf + seq_len, :]
    vo_ref[...] = vv_scr[:, seq_len:wbuf + seq_len, :]

    nq = N_HEADS * seq_len
    l_idx = lax.broadcasted_iota(jnp.int32, (nq, 2 * WINDOW), 0) % seq_len
    kj = lax.broadcasted_iota(jnp.int32, (nq, 2 * WINDOW), 1)
    band = (kj > l_idx) & (kj <= l_idx + wbuf)
    lane = lax.broadcasted_iota(jnp.int32, (1, 1, LANES), 2)
    lane_c = [lane < HEAD_DIM, lane >= HEAD_DIM]

    q3 = q.reshape(nseq, seq_len, MIX_B)
    qg = jnp.concatenate([q3[:, :, g * LANES:(g + 1) * LANES] for g in range(GQA)], axis=1)
    qs = jnp.concatenate([jnp.where(lane_c[c], qg, 0.0) for c in range(N_KV)], axis=1).astype(BF16)
    kk = kk_scr[...].astype(BF16)
    vv = vv_scr[...]
    s = jnp.einsum('nqe,nke->nqk', qs, kk, preferred_element_type=F32) * (1.0 / math.sqrt(HEAD_DIM))
    s = s + bias_ref[...][None]
    p = _swa_softmax(s, band[None], sink_ref[...][None]).astype(BF16)
    half = GQA * seq_len
    o = (jnp.einsum('nqk,nke->nqe', p[:, :half], jnp.where(lane_c[0], vv, 0.0).astype(BF16),
                    preferred_element_type=F32)
         + jnp.einsum('nqk,nke->nqe', p[:, half:], jnp.where(lane_c[1], vv, 0.0).astype(BF16),
                      preferred_element_type=F32))
    ob3 = jnp.concatenate([o[:, g * seq_len:(g + 1) * seq_len, :] for g in range(GQA)], axis=2)
    ob = ob3.reshape(nseq * seq_len, MIX_B)
    ya_n = _rms(ya, ga_ref[...]).astype(BF16)
    ob_n = _rms(ob, gb_ref[...]).astype(BF16)
    h1_ref[...] = x + _dot(ya_n, woa_ref[...]) + _dot(ob_n, wob_ref[...])


def _mem_kv_kernel(mem_ref, g_ref, wk_ref, wv_ref, mk_ref, mv_ref):
    mn = _rms(mem_ref[0], g_ref[...]).astype(BF16)
    mk_ref[0] = _dot(mn, wk_ref[...])
    mv_ref[0] = _dot(mn, wv_ref[...])


def _mem_tail(h1, o, wmo_ref, gffn_ref, wq_ref, h2_ref, xn_ref, qp_ref):
    h2 = h1 + _dot(o.astype(BF16), wmo_ref[...])
    h2_ref[...] = h2
    xn = _rms(h2, gffn_ref[...]).astype(BF16)
    xn_ref[...] = xn
    qp = _dot(xn, wq_ref[...]).astype(BF16)
    for hp in range(2 * PEER_HEADS):
        qp_ref[hp] = qp[:, hp * PEER_HALF:(hp + 1) * PEER_HALF]


def _mem_prompt_kernel(h1_ref, mk_ref, mv_ref, gmem_ref, wmq_ref, wmo_ref, gffn_ref, wq_ref,
                       h2_ref, xn_ref, qp_ref):
    h1 = h1_ref[...]
    q = _dot(_rms(h1, gmem_ref[...]).astype(BF16), wmq_ref[...]).astype(BF16)
    mk = mk_ref[0].astype(BF16)
    mv = mv_ref[0].astype(BF16)
    def head_cols(hh):
        return slice(hh * MEM_HEAD_DIM, (hh + 1) * MEM_HEAD_DIM)

    def scores(hh):
        c = head_cols(hh)
        return _dot_nt(q[:, c], mk[:, c]) * (1.0 / math.sqrt(MEM_HEAD_DIM))

    outs = []
    s_next = scores(0)
    for hh in range(MEM_HEADS):
        c = head_cols(hh)
        s = s_next
        if hh + 1 < MEM_HEADS:
            s_next = scores(hh + 1)
        m = jnp.max(s, axis=-1, keepdims=True)
        p = jnp.exp(s - m)
        p = (p / jnp.sum(p, axis=-1, keepdims=True)).astype(BF16)
        outs.append(_dot(p, mv[:, c]))
    o = jnp.concatenate(outs, axis=1)
    _mem_tail(h1, o, wmo_ref, gffn_ref, wq_ref, h2_ref, xn_ref, qp_ref)


def _mem_sample_kernel(h1_ref, ck_ref, cv_ref, gmem_ref, wmq_ref, wmo_ref, gffn_ref, wq_ref,
                       h2_ref, xn_ref, qp_ref, *, nseq, seq_len):
    h1 = h1_ref[...]
    q = _dot(_rms(h1, gmem_ref[...]).astype(BF16), wmq_ref[...])
    q3 = q.reshape(nseq, seq_len, MEM_INNER)
    qs = jnp.concatenate([q3[:, :, hh * MEM_HEAD_DIM:(hh + 1) * MEM_HEAD_DIM] for hh in range(MEM_HEADS)],
                         axis=1).astype(BF16)
    ck = ck_ref[...].astype(BF16)
    cv = cv_ref[...].astype(BF16)
    s = jnp.einsum('nqd,nkd->nqk', qs, ck, preferred_element_type=F32) * (1.0 / math.sqrt(MEM_HEAD_DIM))
    nq, nk = MEM_HEADS * seq_len, N_MEM * MEM_HEADS
    row_head = lax.broadcasted_iota(jnp.int32, (nq, nk), 0) // seq_len
    col_head = lax.broadcasted_iota(jnp.int32, (nq, nk), 1) % MEM_HEADS
    s = jnp.where((row_head == col_head)[None], s, -jnp.inf)
    m = jnp.max(s, axis=-1, keepdims=True)
    p = jnp.exp(s - m)
    p = (p / jnp.sum(p, axis=-1, keepdims=True)).astype(BF16)
    of = jnp.einsum('nqk,nkd->nqd', p, cv, preferred_element_type=F32)
    o3 = jnp.concatenate([of[:, hh * seq_len:(hh + 1) * seq_len] for hh in range(MEM_HEADS)], axis=2)
    o = o3.reshape(nseq * seq_len, MEM_INNER)
    _mem_tail(h1, o, wmo_ref, gffn_ref, wq_ref, h2_ref, xn_ref, qp_ref)


def _sort_network(n):
    out, p = [], 1
    while p < n:
        k = p
        while k >= 1:
            for j in range(k % p, n - k, 2 * k):
                for i in range(min(k, n - j - k)):
                    if (i + j) // (2 * p) == (i + j + k) // (2 * p):
                        out.append((i + j, i + j + k))
            k //= 2
        p *= 2
    return out


_SORT16 = _sort_network(PEER_TOPK)


def _compare_exchange(x, i, j):
    hi, lo = jnp.maximum(x[i], x[j]), jnp.minimum(x[i], x[j])
    x[i], x[j] = hi, lo


def _top16_sorted(slabs):
    n = len(slabs)
    x = list(slabs)
    for i, j in _SORT16:
        if j < n:
            _compare_exchange(x, i, j)
    for shift in (4, 2, 1):
        y = []
        for i in range(PEER_TOPK):
            lo = x[i] if i < n else None
            k = PEER_TOPK - 1 - i
            hi = pltpu.roll(x[k], shift, axis=0) if k < n else None
            y.append(hi if lo is None else lo if hi is None else jnp.maximum(lo, hi))
        d = PEER_TOPK // 2
        while d >= 1:
            for i in range(PEER_TOPK):
                if i & d == 0:
                    _compare_exchange(y, i, i + d)
            d //= 2
        x, n = y, PEER_TOPK
    return x


def _step_count(slabs, thresholds, below):
    th = thresholds
    assert len(th) == PEER_TOPK == 16
    hit = (lambda x, t: x < t) if below else (lambda x, t: x >= t)
    out = []
    for x in slabs:
        c8 = hit(x, th[7])
        c4 = hit(x, jnp.where(c8, th[11], th[3]))
        c2 = hit(x, jnp.where(c8, jnp.where(c4, th[13], th[9]), jnp.where(c4, th[5], th[1])))
        upper = jnp.where(c4, jnp.where(c2, th[14], th[12]), jnp.where(c2, th[10], th[8]))
        lower = jnp.where(c4, jnp.where(c2, th[6], th[4]), jnp.where(c2, th[2], th[0]))
        c1 = hit(x, jnp.where(c8, upper, lower))
        count = (jnp.where(c8, 8.0, 0.0) + jnp.where(c4, 4.0, 0.0)) + (jnp.where(c2, 2.0, 0.0) + jnp.where(c1, 1.0, 0.0))
        out.append(jnp.where(hit(x, th[15]), 16.0, count))
    return out


def _peer_prep_kernel(qp_ref, keys_ref, r2_ref, e2_ref, cnt_ref, e1_ref, *, lane_tiles):
    nslab = PEER_KEYS // SUBLANES

    def per_head(h, carry):
        for lt in range(lane_tiles):
            lanes = slice(lt * LANES, (lt + 1) * LANES)
            q1 = qp_ref[2 * h, lanes, :]
            q2 = qp_ref[2 * h + 1, lanes, :]
            s1 = _dot_nt(keys_ref[h, 0], q1)
            s2 = _dot_nt(keys_ref[h, 1], q2)
            s1s = [s1[r * SUBLANES:(r + 1) * SUBLANES] for r in range(nslab)]
            s2s = [s2[r * SUBLANES:(r + 1) * SUBLANES] for r in range(nslab)]
            a = _top16_sorted(s1s)
            b = _top16_sorted(s2s)
            sub = lax.broadcasted_iota(jnp.int32, a[0].shape, 0)

            def by_sublane(vals):
                out = vals[SUBLANES - 1]
                for r in range(SUBLANES - 2, -1, -1):
                    out = jnp.where(sub == r, vals[r], out)
                return out

            a_lo, a_hi = by_sublane(a[:8]), by_sublane(a[8:])
            b_lo, b_hi = by_sublane(b[:8]), by_sublane(b[8:])
            tail = lambda x: jnp.where(sub >= 3, x, -jnp.inf)
            cand = [a_lo + b[0], a_hi + b[0], a_lo + b[1], a_lo + b[2], tail(a[0] + b_lo), a[0] + b_hi,
                    tail(a[1] + b_lo), tail(a[2] + b_lo), tail(a[3] + b_lo)]
            tops = _top16_sorted(cand)
            tau = tops[-1]
            z = jnp.ones_like(tau)
            for t in tops[1:]:
                z = z + jnp.exp(t - tops[0])
            thr = []
            for j in range(PEER_TOPK):
                t = jnp.full_like(tau, jnp.inf)
                for i in range(PEER_TOPK // (j + 1)):
                    t = jnp.where(a[i] + b[j] >= tau, a[i], t)
                thr.append(t)
            cnt = _step_count(s1s, thr, below=False)
            rank2 = _step_count(s2s, b, below=True)
            half_inv_z = 0.5 / z
            r2_ref[h, :, lanes] = jnp.concatenate(rank2, axis=0).astype(BF16)
            e2_ref[h, :, lanes] = jnp.concatenate([jnp.exp(x - b[0]) * half_inv_z for x in s2s], axis=0).astype(BF16)
            cnt_ref[h, :, lanes] = jnp.concatenate(cnt, axis=0)
            e1_ref[h, :, lanes] = jnp.concatenate([jnp.exp(x - a[0]) for x in s1s], axis=0)
        return carry

    lax.fori_loop(0, PEER_HEADS, per_head, 0)


def _peer_weight_rows(j, row, lanes, cnt_slab, e1_slab, x_ref, p_ref, r2_ref, e2_ref):
    sub, width = SUBLANES, lanes.stop - lanes.start
    rows = slice(j * PEER_KEYS, (j + 1) * PEER_KEYS)
    x = x_ref[rows, lanes]
    c0 = math.sqrt(2.0 / math.pi)
    act = (x + x * jnp.tanh(x * ((0.044715 * c0) * (x * x) + c0))).astype(BF16)
    w = None
    for h in range(PEER_HEADS):
        cnt = jnp.broadcast_to(cnt_slab[h][row:row + 1, lanes], (sub, width)).astype(BF16)
        e1 = jnp.broadcast_to(e1_slab[h][row:row + 1, lanes], (sub, width)).astype(BF16)
        cnt = jnp.concatenate([cnt] * (PEER_KEYS // sub), axis=0)
        e1 = jnp.concatenate([e1] * (PEER_KEYS // sub), axis=0)
        e2 = e2_ref[h, :, lanes]
        term = jnp.where(r2_ref[h, :, lanes] < cnt, e2, jnp.zeros_like(e2)) * e1
        w = term if w is None else w + term
    p_ref[rows, lanes] = w * act


def _interleave(mxu_chunks, vpu_chunks):
    per = -(-len(vpu_chunks) // max(len(mxu_chunks), 1))
    for c, mxu in enumerate(mxu_chunks):
        mxu()
        for vpu in vpu_chunks[c * per:(c + 1) * per]:
            vpu()
    for vpu in vpu_chunks[len(mxu_chunks) * per:]:
        vpu()


def _peer_ffn_kernel(xn_ref, h2_ref, eu_ref, evt_ref, r2_ref, e2_ref, cnt_ref, e1_ref, gfin_ref,
                     y_ref, acc_ref, x_a, x_b, p_a, p_b, *, tokens, i1_per_step, n_parts):
    e = pl.program_id(1)
    part_keys = i1_per_step // n_parts
    part = part_keys * PEER_KEYS
    width = 2 * LANES
    n_lt = tokens // width
    d_rows = D_MODEL // 2
    xs, ps = (x_a, x_b), (p_a, p_b)

    @pl.when(e == 0)
    def _():
        acc_ref[...] = jnp.zeros_like(acc_ref)

    i1_0 = pl.multiple_of(e * i1_per_step, i1_per_step)
    cnt_slab = [cnt_ref[h, pl.ds(i1_0, i1_per_step), :] for h in range(PEER_HEADS)]
    e1_slab = [e1_ref[h, pl.ds(i1_0, i1_per_step), :] for h in range(PEER_HEADS)]

    def u_product(t, lt):
        lanes = slice(lt * width, (lt + 1) * width)
        xs[t % 2][:, lanes] = _dot_nt(eu_ref[t * part:(t + 1) * part, :], xn_ref[lanes, :])

    def v_product(t, dh, lt):
        rows = slice(dh * d_rows, (dh + 1) * d_rows)
        lanes = slice(lt * width, (lt + 1) * width)
        acc_ref[rows, lanes] += _dot(evt_ref[rows, t * part:(t + 1) * part], ps[t % 2][:, lanes])

    def weights(t):
        return [functools.partial(_peer_weight_rows, j, t * part_keys + j, slice(lt * width, (lt + 1) * width),
                                  cnt_slab, e1_slab, xs[t % 2], ps[t % 2], r2_ref, e2_ref)
                for lt in range(n_lt) for j in range(part_keys)]

    for t in range(n_parts + 2):
        mxu = []
        if t < n_parts:
            mxu += [functools.partial(u_product, t, lt) for lt in range(n_lt)]
        if t >= 2:
            mxu += [functools.partial(v_product, t - 2, dh, lt) for lt in range(n_lt) for dh in range(2)]
        _interleave(mxu, weights(t - 1) if 1 <= t <= n_parts else [])

    @pl.when(e == pl.num_programs(1) - 1)
    def _():
        y_ref[...] = _rms(h2_ref[...] + acc_ref[...].T, gfin_ref[...])


def _full(shape):
    return pl.BlockSpec(shape, lambda *_: (0,) * len(shape))


def _t5_bucket(d):
    n = np.maximum(d, 0)
    nf = np.maximum(n, 1).astype(np.float32)
    scaled = (np.log(nf / np.float32(REL_MAX_EXACT)) / np.float32(math.log(REL_MAX_DIST / REL_MAX_EXACT))
              * np.float32(REL_BUCKETS - REL_MAX_EXACT))
    frac = np.abs(scaled - np.round(scaled))
    capped = scaled > REL_BUCKETS - REL_MAX_EXACT
    assert np.all((frac > 1e-3) | (frac < 1e-5) | capped), "bucket boundary too close to an integer distance"
    large = np.minimum(REL_MAX_EXACT + np.floor(scaled + 1e-4).astype(np.int32), REL_BUCKETS - 1)
    return np.where(n < REL_MAX_EXACT, n, large).astype(np.int32)


def _bucket_table(n_q):
    d = (np.arange(n_q)[:, None] + WINDOW) - np.arange(2 * WINDOW)[None, :]
    return jnp.asarray(_t5_bucket(d))


def _head_perm():
    g, c, d = jnp.meshgrid(jnp.arange(GQA), jnp.arange(N_KV), jnp.arange(HEAD_DIM), indexing='ij')
    return ((c * GQA + g) * HEAD_DIM + d).reshape(-1)


def _mix_weights(norm_mix_g, w_in, ln_v_g, ln_v_b, norm_a_out_g, norm_b_out_g, w_out):
    perm = _head_perm()
    q0 = 2 * MIX_A
    win = jnp.concatenate([w_in[:, :q0], w_in[:, q0:q0 + MIX_B][:, perm], w_in[:, q0 + MIX_B:]], axis=1).astype(BF16)
    woa = w_out[:MIX_A].astype(BF16)
    wob = w_out[MIX_A:][perm].astype(BF16)
    gb = norm_b_out_g[perm][None]
    return (norm_mix_g[None], win, ln_v_g[None], ln_v_b[None], norm_a_out_g[None], gb, woa, wob)


def _mix_prompt(x, wts, wm, bs, bkt, relb, sink, *, rows, seq):
    gmix, win, lng, lnb, ga, gb, woa, wob = wts
    t = x.shape[0]
    row_spec = lambda w: pl.BlockSpec((rows, w), lambda i: (i, 0))
    kern = functools.partial(_mix_prompt_kernel, rows=rows, tiles_per_seq=seq // rows)
    return pl.pallas_call(
        kern,
        grid=(t // rows,),
        in_specs=[row_spec(D_MODEL), _full(gmix.shape), _full(win.shape), _full(lng.shape), _full(lnb.shape),
                  _full(wm.shape), _full(bs.shape), _full(bkt.shape), _full(relb.shape), _full(sink.shape),
                  _full(ga.shape), _full(gb.shape), _full(woa.shape), _full(wob.shape)],
        out_specs=[row_spec(D_MODEL), row_spec(LANES), row_spec(LANES)],
        out_shape=[jax.ShapeDtypeStruct((t, D_MODEL), F32), jax.ShapeDtypeStruct((t, LANES), F32),
                   jax.ShapeDtypeStruct((t, LANES), F32)],
        scratch_shapes=[pltpu.VMEM((WINDOW, LANES), F32), pltpu.VMEM((WINDOW, LANES), F32),
                        pltpu.VMEM((rows, MIX_A), F32), pltpu.VMEM((rows, MIX_B), F32),
                        pltpu.VMEM((N_HEADS * WINDOW, 2 * WINDOW), F32)],
        compiler_params=_cparams(1),
        name="mix_prompt",
    )(x, gmix, win, lng, lnb, wm, bs, bkt, relb, sink, ga, gb, woa, wob)


def _mix_sample(x, ck, cv, wts, wm, bs, bkt, relb, sink, *, nseq, seq_len):
    gmix, win, lng, lnb, ga, gb, woa, wob = wts
    t = x.shape[0]
    rows = nseq * seq_len
    row_spec = lambda w: pl.BlockSpec((rows, w), lambda i: (i, 0))
    seq_spec = pl.BlockSpec((nseq, WINDOW, LANES), lambda i: (i, 0, 0))
    kern = functools.partial(_mix_sample_kernel, nseq=nseq, seq_len=seq_len)
    return pl.pallas_call(
        kern,
        grid=(t // rows,),
        in_specs=[row_spec(D_MODEL), seq_spec, seq_spec, _full(gmix.shape), _full(win.shape), _full(lng.shape),
                  _full(lnb.shape), _full(wm.shape), _full(bs.shape), _full(bkt.shape), _full(relb.shape),
                  _full(sink.shape),
                  _full(ga.shape), _full(gb.shape), _full(woa.shape), _full(wob.shape)],
        out_specs=[row_spec(D_MODEL), seq_spec, seq_spec, row_spec(MIX_A)],
        out_shape=[jax.ShapeDtypeStruct((t, D_MODEL), F32),
                   jax.ShapeDtypeStruct(ck.shape, F32), jax.ShapeDtypeStruct(cv.shape, F32),
                   jax.ShapeDtypeStruct((t, MIX_A), F32)],
        scratch_shapes=[pltpu.VMEM((nseq, 2 * WINDOW, LANES), F32), pltpu.VMEM((nseq, 2 * WINDOW, LANES), F32),
                        pltpu.VMEM((N_HEADS * seq_len, 2 * WINDOW), F32)],
        compiler_params=_cparams(1),
        name="mix_sample",
    )(x, ck, cv, gmix, win, lng, lnb, wm, bs, bkt, relb, sink, ga, gb, woa, wob)


def _mem_kv(mem, g, wk, wv):
    b, m, d = mem.shape
    return pl.pallas_call(
        _mem_kv_kernel,
        grid=(b,),
        in_specs=[pl.BlockSpec((1, m, d), lambda i: (i, 0, 0)), _full(g.shape), _full(wk.shape), _full(wv.shape)],
        out_specs=[pl.BlockSpec((1, m, MEM_INNER), lambda i: (i, 0, 0))] * 2,
        out_shape=[jax.ShapeDtypeStruct((b, m, MEM_INNER), F32)] * 2,
        compiler_params=_cparams(1),
        name="mem_kv",
    )(mem, g, wk, wv)


def _mem_outs(t, rows):
    row_spec = lambda w: pl.BlockSpec((rows, w), lambda i: (i, 0))
    specs = [row_spec(D_MODEL), row_spec(D_MODEL),
             pl.BlockSpec((2 * PEER_HEADS, rows, PEER_HALF), lambda i: (0, i, 0))]
    shapes = [jax.ShapeDtypeStruct((t, D_MODEL), F32), jax.ShapeDtypeStruct((t, D_MODEL), BF16),
              jax.ShapeDtypeStruct((2 * PEER_HEADS, t, PEER_HALF), BF16)]
    return specs, shapes


def _mem_prompt(h1, mk, mv, mwts, *, rows, seq):
    t = h1.shape[0]
    tiles_per_seq = seq // rows
    out_specs, out_shapes = _mem_outs(t, rows)
    kv_spec = pl.BlockSpec((1, N_MEM, MEM_INNER), lambda i: (i // tiles_per_seq, 0, 0))
    return pl.pallas_call(
        _mem_prompt_kernel,
        grid=(t // rows,),
        in_specs=[pl.BlockSpec((rows, D_MODEL), lambda i: (i, 0)), kv_spec, kv_spec] + [_full(w.shape) for w in mwts],
        out_specs=out_specs,
        out_shape=out_shapes,
        compiler_params=_cparams(1),
        name="mem_prompt",
    )(h1, mk, mv, *mwts)


def _mem_sample(h1, ck, cv, mwts, *, nseq, seq_len):
    t = h1.shape[0]
    rows = nseq * seq_len
    out_specs, out_shapes = _mem_outs(t, rows)
    kv_spec = pl.BlockSpec((nseq, N_MEM * MEM_HEADS, MEM_HEAD_DIM), lambda i: (i, 0, 0))
    kern = functools.partial(_mem_sample_kernel, nseq=nseq, seq_len=seq_len)
    return pl.pallas_call(
        kern,
        grid=(t // rows,),
        in_specs=[pl.BlockSpec((rows, D_MODEL), lambda i: (i, 0)), kv_spec, kv_spec] + [_full(w.shape) for w in mwts],
        out_specs=out_specs,
        out_shape=out_shapes,
        compiler_params=_cparams(1),
        name="mem_sample",
    )(h1, ck, cv, *mwts)


def _peer_prep(qp, keys, *, tokens):
    t = qp.shape[1]
    out_spec = pl.BlockSpec((PEER_HEADS, PEER_KEYS, tokens), lambda i: (0, 0, i))
    kern = functools.partial(_peer_prep_kernel, lane_tiles=tokens // LANES)
    return pl.pallas_call(
        kern,
        grid=(t // tokens,),
        in_specs=[pl.BlockSpec((2 * PEER_HEADS, tokens, PEER_HALF), lambda i: (0, i, 0)), _full(keys.shape)],
        out_specs=[out_spec] * 4,
        out_shape=[jax.ShapeDtypeStruct((PEER_HEADS, PEER_KEYS, t), dt) for dt in (BF16, BF16, F32, F32)],
        compiler_params=_cparams(1),
        name="peer_prep",
    )(qp, keys)


def _peer_ffn(xn, h2, eu, evt, prep, gfin, *, tokens, i1_per_step):
    t = xn.shape[0]
    experts = i1_per_step * PEER_KEYS
    tok_spec = pl.BlockSpec((tokens, D_MODEL), lambda i, e: (i, 0))
    prep_spec = pl.BlockSpec((PEER_HEADS, PEER_KEYS, tokens), lambda i, e: (0, 0, i))
    n_parts = 4
    part = experts // n_parts
    kern = functools.partial(_peer_ffn_kernel, tokens=tokens, i1_per_step=i1_per_step, n_parts=n_parts)
    return pl.pallas_call(
        kern,
        grid=(t // tokens, PEER_N // experts),
        in_specs=[tok_spec, tok_spec,
                  pl.BlockSpec((experts, D_MODEL), lambda i, e: (e, 0)),
                  pl.BlockSpec((D_MODEL, experts), lambda i, e: (0, e)),
                  prep_spec, prep_spec, prep_spec, prep_spec,
                  pl.BlockSpec(gfin.shape, lambda i, e: (0, 0))],
        out_specs=tok_spec,
        out_shape=jax.ShapeDtypeStruct((t, D_MODEL), F32),
        scratch_shapes=[pltpu.VMEM((D_MODEL, tokens), F32),
                        pltpu.VMEM((part, tokens), F32), pltpu.VMEM((part, tokens), F32),
                        pltpu.VMEM((part, tokens), BF16), pltpu.VMEM((part, tokens), BF16)],
        compiler_params=_cparams(2),
        name="peer_ffn",
    )(xn, h2, eu, evt, *prep, gfin)


def kernel(x_prompt, x_sample, mem_prompt, cache_swa_k, cache_swa_v, cache_mem_k, cache_mem_v, norm_mix_g, w_in, ln_v_g, ln_v_b, spatial_w, spatial_b, attn_sinks, rel_bias, norm_a_out_g, norm_b_out_g, w_out, norm_mem_g, norm_memsrc_g, w_mq, w_mk, w_mv, w_mo, norm_ffn_g, peer_wq, peer_keys, peer_u, peer_v, norm_final_g):
    batch, seq, _ = x_prompt.shape
    nsamp, dec_len, _ = x_sample.shape
    l = 0

    wts = _mix_weights(norm_mix_g[l], w_in[l], ln_v_g[l], ln_v_b[l],
                       norm_a_out_g[l], norm_b_out_g[l], w_out[l])
    tril = jnp.tril(jnp.ones((CHUNK, CHUNK), F32))
    wm = spatial_w[l] * tril
    wm_p = wm.astype(BF16)
    bs_p = spatial_b[l][:, :, None]
    seqs_per_chunk = CHUNK // dec_len
    eye = jnp.eye(seqs_per_chunk, dtype=F32)
    wm_s = jnp.einsum('ab,gts->gatbs', eye, wm[:, :dec_len, :dec_len]).reshape(A_GROUPS, CHUNK, CHUNK).astype(BF16)
    bs_s = jnp.tile(spatial_b[l][:, :dec_len], (1, seqs_per_chunk))[:, :, None]
    bkt_p = _bucket_table(WINDOW)
    bkt_s = _bucket_table(dec_len)
    sink_p = jnp.repeat(attn_sinks[l], WINDOW)[:, None]
    sink_s = jnp.repeat(attn_sinks[l], dec_len)[:, None]
    mwts = (norm_mem_g[l][None], w_mq[l].astype(BF16), w_mo[l].astype(BF16), norm_ffn_g[l][None],
            peer_wq[l].astype(BF16))
    keys = peer_keys[l].astype(BF16)
    eu = peer_u[l].astype(BF16)
    evt = peer_v[l].T.astype(BF16)
    gfin = norm_final_g[None]

    xp = x_prompt.reshape(batch * seq, D_MODEL)
    h1p, kp, vp = _mix_prompt(xp, wts, wm_p, bs_p, bkt_p, rel_bias, sink_p, rows=1024, seq=seq)
    mk, mv = _mem_kv(mem_prompt, norm_memsrc_g[l][None], w_mk[l].astype(BF16), w_mv[l].astype(BF16))
    h2p, xnp_, qpp = _mem_prompt(h1p, mk, mv, mwts, rows=1024, seq=seq)
    prep_p = _peer_prep(qpp, keys, tokens=1024)
    yp = _peer_ffn(xnp_, h2p, eu, evt, prep_p, gfin, tokens=512, i1_per_step=16)

    xs = x_sample.reshape(nsamp * dec_len, D_MODEL)
    ck = cache_swa_k.reshape(nsamp, WINDOW, LANES)
    cv = cache_swa_v.reshape(nsamp, WINDOW, LANES)
    h1s, ks, vs, chunk_v = _mix_sample(xs, ck, cv, wts, wm_s, bs_s, bkt_s, rel_bias, sink_s,
                                       nseq=seqs_per_chunk, seq_len=dec_len)
    cmk = cache_mem_k.reshape(nsamp, N_MEM * MEM_HEADS, MEM_HEAD_DIM)
    cmv = cache_mem_v.reshape(nsamp, N_MEM * MEM_HEADS, MEM_HEAD_DIM)
    h2s, xns, qps = _mem_sample(h1s, cmk, cmv, mwts, nseq=8, seq_len=dec_len)
    prep_s = _peer_prep(qps, keys, tokens=1024)
    ys = _peer_ffn(xns, h2s, eu, evt, prep_s, gfin, tokens=512, i1_per_step=16)

    kp4 = kp.reshape(batch, seq, LANES)[:, -WINDOW:].reshape(batch, WINDOW, N_KV, HEAD_DIM)
    vp4 = vp.reshape(batch, seq, LANES)[:, -WINDOW:].reshape(batch, WINDOW, N_KV, HEAD_DIM)
    return (yp.reshape(batch, seq, D_MODEL),
            ys.reshape(nsamp, dec_len, D_MODEL),
            kp4[None], vp4[None],
            mk.reshape(batch, N_MEM, MEM_HEADS, MEM_HEAD_DIM)[None],
            mv.reshape(batch, N_MEM, MEM_HEADS, MEM_HEAD_DIM)[None],
            ks.reshape(nsamp, WINDOW, N_KV, HEAD_DIM)[None],
            vs.reshape(nsamp, WINDOW, N_KV, HEAD_DIM)[None],
            chunk_v.reshape(nsamp, dec_len, A_GROUPS, A_GW)[None])
```
